```python
import jax, jax.numpy as jnp
from jax import lax
import numpy as np

D_MODEL = 1024
BATCH = 4
SEQ = 4096
DEPTH = 1
DEC_BATCH = 32
DEC_SEQ = 1
PAST_LEN = 8192
PAGE_SIZE = 128

N_HEADS_NSA = 8
HEAD_DIM = 64
N_KV_HEADS = 2
GQA_GROUP = N_HEADS_NSA // N_KV_HEADS
CMP_BLOCK = 32
CMP_STRIDE = 16
CMP_HIDDEN = 256
SEL_BLOCK = 64
SEL_TOPK = 16
WINDOW = 512
N_BRANCH = 3
HG_HEADS = 4
HG_DK = 128
HG_DV = 128
HG_CHUNK = 32
NSA_WIDTH = N_HEADS_NSA * HEAD_DIM
HG_WIDTH = HG_HEADS * HG_DK
HG_VWIDTH = HG_HEADS * HG_DV
MIX_WIDTH = NSA_WIDTH + HG_VWIDTH
KV_WIDTH = 2 * N_KV_HEADS * HEAD_DIM
IN_SPLITS = (NSA_WIDTH, KV_WIDTH, KV_WIDTH, KV_WIDTH, N_BRANCH * N_HEADS_NSA,
             HG_WIDTH, HG_WIDTH, HG_VWIDTH, HG_VWIDTH)
IN_WIDTH = sum(IN_SPLITS)
D_FF = -(-8 * D_MODEL // (3 * 256)) * 256
D_PLE = 256
SEL_QBLOCK = 64
WIN_QBLOCK = 128
SCALE = HEAD_DIM ** -0.5
NEG_INF = -1e30
FORCED_SCORE = 1e9
EPS = 1e-6

kernel_name = "nsa_hgrn2_parallel_heads_decode_step"


def rms_norm(x, g):
    xf = x.astype(jnp.float32)
    y = xf * lax.rsqrt(jnp.mean(xf * xf, axis=-1, keepdims=True) + EPS) * g.astype(jnp.float32)
    return y.astype(x.dtype)


def masked_softmax(s, mask):
    s = jnp.where(mask, s.astype(jnp.float32), NEG_INF)
    return jax.nn.softmax(s, axis=-1) * mask


def compress_kv(kv, pos, w1, w2):
    B, L = kv.shape[:2]
    n_half = L // CMP_STRIDE
    halves = kv[:, :n_half * CMP_STRIDE].reshape(B, n_half, CMP_STRIDE, 2, N_KV_HEADS, HEAD_DIM)
    w1h = w1.reshape(2, 2, CMP_STRIDE, HEAD_DIM, CMP_HIDDEN)
    posh = pos.reshape(2, 2, CMP_STRIDE, HEAD_DIM).transpose(1, 2, 0, 3)
    h_lo = jnp.einsum('bnschd,csdf->bnchf', halves + posh[0][:, :, None, :], w1h[:, 0])
    h_hi = jnp.einsum('bnschd,csdf->bnchf', halves + posh[1][:, :, None, :], w1h[:, 1])
    hid = jax.nn.silu(h_lo[:, :-1] + h_hi[:, 1:])
    return jnp.einsum('bnchf,cfd->bnchd', hid, w2)


def compressed_attention(q, kv_blocks, qpos):
    n_cmp = kv_blocks.shape[1]
    s = jnp.einsum('btkgd,bnkd->bkgtn', q, kv_blocks[:, :, 0]) * SCALE
    blk_end = jnp.arange(n_cmp) * CMP_STRIDE + (CMP_BLOCK - 1)
    p = masked_softmax(s, blk_end[None, :] <= qpos[:, None])
    o = jnp.einsum('bkgtn,bnkd->btkgd', p, kv_blocks[:, :, 1])
    return o, p


def select_blocks(p_cmp, qpos, L):
    n_cmp = p_cmp.shape[-1]
    n_slc = -(-L // SEL_BLOCK)
    cs = jnp.arange(n_cmp) * CMP_STRIDE
    ss = jnp.arange(n_slc) * SEL_BLOCK
    overlap = ((cs[:, None] < ss[None, :] + SEL_BLOCK) &
               (cs[:, None] + CMP_BLOCK > ss[None, :])).astype(p_cmp.dtype)
    imp = jnp.einsum('bkgtn,ns->bkts', p_cmp, overlap)
    blk = jnp.arange(n_slc)[None, :]
    qblk = (qpos // SEL_BLOCK)[:, None]
    forced = (blk == 0) | (blk == qblk) | (blk == qblk - 1)
    valid = blk * SEL_BLOCK <= qpos[:, None]
    score = jnp.where(valid, jnp.where(forced, FORCED_SCORE, imp), NEG_INF)
    top_val, top_idx = lax.top_k(score, min(SEL_TOPK, n_slc))
    return top_idx, top_val > 0.5 * NEG_INF


def selected_attention(q, kv_all, top_idx, sel_ok, qpos):
    B, T = q.shape[:2]
    L = kv_all.shape[1]
    K = top_idx.shape[-1]
    kv_t = kv_all.transpose(0, 3, 1, 2, 4)
    qb = SEL_QBLOCK if T % SEL_QBLOCK == 0 else T
    nb = T // qb
    q_b = jnp.moveaxis(q.reshape(B, nb, qb, N_KV_HEADS, GQA_GROUP, HEAD_DIM), 1, 0)
    idx_b = jnp.moveaxis(top_idx.reshape(B, N_KV_HEADS, nb, qb, K), 2, 0)
    ok_b = jnp.moveaxis(sel_ok.reshape(B, N_KV_HEADS, nb, qb, K), 2, 0)
    pos_b = qpos.reshape(nb, qb)
    gather_rows = jax.vmap(jax.vmap(lambda rows, ii: rows[ii]))

    def one_block(args):
        qc, idx, ok, pos = args
        tok = idx[..., None] * SEL_BLOCK + jnp.arange(SEL_BLOCK)
        rows = gather_rows(kv_t, jnp.minimum(tok, L - 1).reshape(B, N_KV_HEADS, qb * K * SEL_BLOCK))
        rows = rows.reshape(B, N_KV_HEADS, qb, K * SEL_BLOCK, 2, HEAD_DIM)
        mask = (ok[..., None] & (tok <= pos[None, None, :, None, None])).reshape(
            B, N_KV_HEADS, qb, K * SEL_BLOCK)
        s = jnp.einsum('bqkgd,bkqnd->bkgqn', qc, rows[..., 0, :]) * SCALE
        p = masked_softmax(s, mask[:, :, None])
        return jnp.einsum('bkgqn,bkqnd->bqkgd', p, rows[..., 1, :])

    o = lax.map(one_block, (q_b, idx_b, ok_b, pos_b))
    return jnp.moveaxis(o, 0, 1).reshape(B, T, N_KV_HEADS, GQA_GROUP, HEAD_DIM)


def window_attention(q, kv_ext, kpos_ext, qpos, n_prefix):
    B, T = q.shape[:2]
    qb = WIN_QBLOCK if T % WIN_QBLOCK == 0 else T
    nb = T // qb
    span = n_prefix + qb

    def one_block(b):
        start = b * qb
        kv = lax.dynamic_slice_in_dim(kv_ext, start, span, axis=1)
        kp = lax.dynamic_slice_in_dim(kpos_ext, start, span)
        qc = lax.dynamic_slice_in_dim(q, start, qb, axis=1)
        qp = lax.dynamic_slice_in_dim(qpos, start, qb)
        dist = qp[:, None] - kp[None, :]
        mask = (dist >= 0) & (dist < WINDOW) & (kp[None, :] >= 0)
        s = jnp.einsum('bqkgd,bnkd->bkgqn', qc, kv[:, :, 0]) * SCALE
        p = masked_softmax(s, mask)
        return jnp.einsum('bkgqn,bnkd->bqkgd', p, kv[:, :, 1])

    o = lax.map(one_block, jnp.arange(nb))
    return jnp.moveaxis(o, 0, 1).reshape(B, T, N_KV_HEADS, GQA_GROUP, HEAD_DIM)


def hgrn2_chunked(q, log_f, v, S0):
    B, T, H, _ = q.shape
    C = HG_CHUNK
    n = -(-T // C)
    pad = n * C - T
    k = -jnp.expm1(log_f.astype(jnp.float32))

    def prep(a):
        a = jnp.pad(a.astype(jnp.float32), ((0, 0), (0, pad), (0, 0), (0, 0)))
        return a.reshape(B, n, C, H, a.shape[-1]).transpose(1, 0, 3, 2, 4)

    causal = jnp.tril(jnp.ones((C, C), dtype=bool))

    def step(S, xs):
        qi, lfi, ki, vi = xs
        cum = jnp.cumsum(lfi, axis=2)
        q_dec = qi * jnp.exp(cum)
        k_inv = ki * jnp.exp(-cum)
        attn = jnp.where(causal, jnp.einsum('bhck,bhsk->bhcs', q_dec, k_inv), 0.0)
        o = jnp.einsum('bhck,bhkv->bhcv', q_dec, S) + jnp.einsum('bhcs,bhsv->bhcv', attn, vi)
        last = cum[:, :, -1:]
        S = jnp.exp(last[:, :, 0])[..., None] * S + jnp.einsum(
            'bhck,bhcv->bhkv', ki * jnp.exp(last - cum), vi)
        return S, o

    S, o = lax.scan(step, S0.astype(jnp.float32), (prep(q), prep(log_f), prep(k), prep(v)))
    o = o.transpose(1, 0, 3, 2, 4).reshape(B, n * C, H, -1)[:, :T]
    return o, S


def hybrid_layer(h, p_emb, qpos, past_cmp, past_slc, win_prefix, win_kpos, S0, lb, buf_len, lw):
    (norm_mix, w_in, cmp_pos, cmp_w1, cmp_w2, norm_nsa_out, norm_hg_out, w_out,
     norm_ffn, w_gate_up, w_down, norm_ple, w_ple_gate, w_ple_proj) = lw
    B, T, _ = h.shape
    xn = rms_norm(h, norm_mix)
    proj = xn @ w_in
    q_a, kvc, kvs, kvw, gate_a, q_b, f_b, v_b, g_b = jnp.split(
        proj, [int(c) for c in np.cumsum(IN_SPLITS)[:-1]], axis=-1)
    q_a = q_a.reshape(B, T, N_KV_HEADS, GQA_GROUP, HEAD_DIM)
    kv_shape = (B, T, 2, N_KV_HEADS, HEAD_DIM)
    kvc, kvs, kvw = kvc.reshape(kv_shape), kvs.reshape(kv_shape), kvw.reshape(kv_shape)
    gate_a = jax.nn.sigmoid(gate_a.astype(jnp.float32)).reshape(
        B, T, N_BRANCH, N_KV_HEADS, GQA_GROUP, 1)
    kvc_all = kvc if past_cmp is None else jnp.concatenate([past_cmp, kvc], axis=1)
    kvs_all = kvs if past_slc is None else jnp.concatenate([past_slc, kvs], axis=1)
    kv_blocks = compress_kv(kvc_all, cmp_pos, cmp_w1, cmp_w2)
    o_cmp, p_cmp = compressed_attention(q_a, kv_blocks, qpos)
    top_idx, sel_ok = select_blocks(p_cmp, qpos, kvs_all.shape[1])
    o_slc = selected_attention(q_a, kvs_all, top_idx, sel_ok, qpos)
    kv_ext = jnp.concatenate([win_prefix, kvw], axis=1)
    o_win = window_attention(q_a, kv_ext, win_kpos, qpos, win_prefix.shape[1])
    o_nsa = (gate_a[:, :, 0] * o_cmp + gate_a[:, :, 1] * o_slc + gate_a[:, :, 2] * o_win)
    o_nsa = o_nsa.reshape(B, T, NSA_WIDTH).astype(h.dtype)
    q_b = jax.nn.silu(q_b).reshape(B, T, HG_HEADS, HG_DK)
    f = lb + (1.0 - lb) * jax.nn.sigmoid(f_b.astype(jnp.float32))
    log_f = jnp.log(f).reshape(B, T, HG_HEADS, HG_DK)
    o_hg, S_new = hgrn2_chunked(q_b, log_f, v_b.reshape(B, T, HG_HEADS, HG_DV), S0)
    o_hg = rms_norm(o_hg.astype(h.dtype), norm_hg_out.reshape(HG_HEADS, HG_DV)).reshape(B, T, HG_VWIDTH)
    o_hg = o_hg * jax.nn.silu(g_b)
    mix = jnp.concatenate([rms_norm(o_nsa, norm_nsa_out), o_hg], axis=-1)
    h = h + mix @ w_out
    gu = rms_norm(h, norm_ffn) @ w_gate_up
    g, u = jnp.split(gu, 2, axis=-1)
    h = h + (jax.nn.silu(g) * u) @ w_down
    h = h + jax.nn.sigmoid(rms_norm(h, norm_ple) @ w_ple_gate) * (p_emb @ w_ple_proj)
    return h, kvc, kvs, kv_ext[:, -buf_len:], S_new


def setup_inputs(seed: int = 0) -> dict:
    key = jax.random.key(seed)
    ks = jax.random.split(key, 28)
    n_pages = PAST_LEN // PAGE_SIZE
    n_used = DEC_BATCH * n_pages
    n_pool = n_used + n_used // 4
    w_buf = min(WINDOW, PAST_LEN)

    def nrm(k, shape, scale=1.0):
        return jax.random.normal(k, shape, jnp.float32) * scale

    def gain(k, shape):
        return 1.0 + 0.05 * nrm(k, shape)

    kv_page = (DEPTH, n_pool, PAGE_SIZE, 2, N_KV_HEADS, HEAD_DIM)
    perm = jax.random.permutation(ks[6], n_pool)
    page_table = perm[:n_used].reshape(DEC_BATCH, n_pages).astype(jnp.int32)
    return {
        "x_prompt": nrm(ks[0], (BATCH, SEQ, D_MODEL)),
        "x_sample": nrm(ks[1], (DEC_BATCH, DEC_SEQ, D_MODEL)),
        "cache_kv_cmp": nrm(ks[2], kv_page),
        "cache_kv_slc": nrm(ks[3], kv_page),
        "state_kv_win": nrm(ks[4], (DEPTH, DEC_BATCH, w_buf, 2, N_KV_HEADS, HEAD_DIM)),
        "state_hgrn": nrm(ks[5], (DEPTH, DEC_BATCH, HG_HEADS, HG_DK, HG_DV), 0.5),
        "page_table": page_table,
        "p_prompt": nrm(ks[7], (DEPTH, BATCH, SEQ, D_PLE)),
        "p_sample": nrm(ks[8], (DEPTH, DEC_BATCH, DEC_SEQ, D_PLE)),
        "norm_mix": gain(ks[9], (DEPTH, D_MODEL)),
        "w_in": nrm(ks[10], (DEPTH, D_MODEL, IN_WIDTH), D_MODEL ** -0.5),
        "cmp_pos": nrm(ks[11], (DEPTH, 2, CMP_BLOCK, HEAD_DIM), 0.2),
        "cmp_w1": nrm(ks[12], (DEPTH, 2, CMP_BLOCK * HEAD_DIM, CMP_HIDDEN), (CMP_BLOCK * HEAD_DIM) ** -0.5),
        "cmp_w2": nrm(ks[13], (DEPTH, 2, CMP_HIDDEN, HEAD_DIM), CMP_HIDDEN ** -0.5),
        "hg_lb": nrm(ks[14], (DEPTH + 1, HG_WIDTH), 0.1),
        "norm_nsa_out": gain(ks[15], (DEPTH, NSA_WIDTH)),
        "norm_hg_out": gain(ks[16], (DEPTH, HG_VWIDTH)),
        "w_out": nrm(ks[17], (DEPTH, MIX_WIDTH, D_MODEL), MIX_WIDTH ** -0.5),
        "norm_ffn": gain(ks[18], (DEPTH, D_MODEL)),
        "w_gate_up": nrm(ks[19], (DEPTH, D_MODEL, 2 * D_FF), D_MODEL ** -0.5),
        "w_down": nrm(ks[20], (DEPTH, D_FF, D_MODEL), D_FF ** -0.5),
        "norm_ple": gain(ks[21], (DEPTH, D_MODEL)),
        "w_ple_gate": nrm(ks[22], (DEPTH, D_MODEL, D_MODEL), D_MODEL ** -0.5),
        "w_ple_proj": nrm(ks[23], (DEPTH, D_PLE, D_MODEL), D_PLE ** -0.5),
        "norm_final": gain(ks[24], (D_MODEL,)),
    }


def reference(x_prompt, x_sample, cache_kv_cmp, cache_kv_slc, state_kv_win, state_hgrn, page_table,
              p_prompt, p_sample, norm_mix, w_in, cmp_pos, cmp_w1, cmp_w2, hg_lb, norm_nsa_out,
              norm_hg_out, w_out, norm_ffn, w_gate_up, w_down, norm_ple, w_ple_gate, w_ple_proj,
              norm_final):
    n_pages = PAST_LEN // PAGE_SIZE
    past_len = n_pages * PAGE_SIZE
    w_buf = state_kv_win.shape[2]
    lb_all = jnp.cumsum(jax.nn.softmax(hg_lb.astype(jnp.float32), axis=0), axis=0)
    qpos_p = jnp.arange(SEQ)
    qpos_s = past_len + jnp.arange(DEC_SEQ)
    kpos_p = jnp.arange(-WINDOW, SEQ)
    kpos_s = past_len - w_buf + jnp.arange(w_buf + DEC_SEQ)
    win_prefix_p = jnp.zeros((BATCH, WINDOW, 2, N_KV_HEADS, HEAD_DIM), x_prompt.dtype)
    S0_p = jnp.zeros((BATCH, HG_HEADS, HG_DK, HG_DV), jnp.float32)
    hp, hs = x_prompt, x_sample
    kvc_p, kvs_p, win_p, hg_p = [], [], [], []
    kvc_s, kvs_s, win_s, hg_s = [], [], [], []
    for l in range(DEPTH):
        lw = (norm_mix[l], w_in[l], cmp_pos[l], cmp_w1[l], cmp_w2[l], norm_nsa_out[l],
              norm_hg_out[l], w_out[l], norm_ffn[l], w_gate_up[l], w_down[l], norm_ple[l],
              w_ple_gate[l], w_ple_proj[l])
        hp, a, b, c, d = hybrid_layer(hp, p_prompt[l], qpos_p, None, None, win_prefix_p, kpos_p,
                                      S0_p, lb_all[l], min(WINDOW, SEQ), lw)
        kvc_p.append(a); kvs_p.append(b); win_p.append(c); hg_p.append(d)
        past_cmp = cache_kv_cmp[l][page_table].reshape(DEC_BATCH, past_len, 2, N_KV_HEADS, HEAD_DIM)
        past_slc = cache_kv_slc[l][page_table].reshape(DEC_BATCH, past_len, 2, N_KV_HEADS, HEAD_DIM)
        hs, a, b, c, d = hybrid_layer(hs, p_sample[l], qpos_s, past_cmp, past_slc, state_kv_win[l],
                                      kpos_s, state_hgrn[l], lb_all[l], w_buf, lw)
        kvc_s.append(a); kvs_s.append(b); win_s.append(c); hg_s.append(d)
    y_prompt = rms_norm(hp, norm_final)
    y_sample = rms_norm(hs, norm_final)
    return (y_prompt, y_sample,
            jnp.stack(kvc_p), jnp.stack(kvs_p), jnp.stack(win_p), jnp.stack(hg_p),
            jnp.stack(kvc_s), jnp.stack(kvs_s), jnp.stack(win_s), jnp.stack(hg_s))
```

```python
import functools

import jax
import jax.numpy as jnp
from jax import lax
from jax.experimental import pallas as pl
from jax.experimental.pallas import tpu as pltpu

F32 = jnp.float32
BF16 = jnp.bfloat16

N_HEADS_NSA = 8
HEAD_DIM = 64
N_KV_HEADS = 2
GQA_GROUP = N_HEADS_NSA // N_KV_HEADS
CMP_BLOCK = 32
CMP_STRIDE = 16
SEL_BLOCK = 64
SEL_TOPK = 16
WINDOW = 512
N_BRANCH = 3
HG_HEADS = 4
HG_DK = 128
HG_DV = 128
HG_CHUNK = 32
SCALE = HEAD_DIM ** -0.5
NEG_INF = -1e30
FORCED_SCORE = 1e9
EPS = 1e-6
BELOW_ALL = -3.0e38

NSA_WIDTH = N_HEADS_NSA * HEAD_DIM
KV_WIDTH = 2 * N_KV_HEADS * HEAD_DIM
HG_WIDTH = HG_HEADS * HG_DK
LANES = 128
V7X_VMEM_LIMIT = 56 * 1024 * 1024


def _cparams(sem, vmem=None):
    return pltpu.CompilerParams(dimension_semantics=sem, vmem_limit_bytes=vmem)


def _sigmoid(x):
    return 1.0 / (1.0 + jnp.exp(-x))


def _silu(x):
    return x * _sigmoid(x)


def _rms(x, g):
    return x * lax.rsqrt(jnp.mean(x * x, axis=-1, keepdims=True) + EPS) * g


def _dot(a, b):
    return jnp.dot(a, b, preferred_element_type=F32)


def _dot_nt(a, b):
    return lax.dot_general(a, b, (((1,), (1,)), ((), ())), preferred_element_type=F32)


def _dot_tn(a, b):
    return lax.dot_general(a, b, (((0,), (0,)), ((), ())), preferred_element_type=F32)


def _split_bf16(x):
    hi = x.astype(BF16)
    lo = (x - hi.astype(F32)).astype(BF16)
    return hi, lo


def _round_up(x, m):
    return -(-x // m) * m


def _fdiv(x, n):
    assert n & (n - 1) == 0
    return jnp.right_shift(x, n.bit_length() - 1)


_OFF_KV = NSA_WIDTH
_OFF_HG = NSA_WIDTH + N_BRANCH * KV_WIDTH
_OFF_GATE = _OFF_HG + 4 * HG_WIDTH
_W_PACKED = _OFF_GATE + N_KV_HEADS * LANES


def _pack_w_in(w_in):
    d = w_in.shape[0]
    a = w_in[:, :_OFF_HG]
    gate = w_in[:, _OFF_HG:_OFF_HG + N_BRANCH * N_HEADS_NSA]
    b = w_in[:, _OFF_HG + N_BRANCH * N_HEADS_NSA:]
    gate = gate.reshape(d, N_BRANCH, N_KV_HEADS, GQA_GROUP).transpose(0, 2, 1, 3)
    gate = gate.reshape(d, N_KV_HEADS, N_BRANCH * GQA_GROUP)
    gate = jnp.pad(gate, ((0, 0), (0, 0), (0, LANES - N_BRANCH * GQA_GROUP)))
    return jnp.concatenate([a, b, gate.reshape(d, N_KV_HEADS * LANES)], axis=1).astype(BF16)


def _inproj_body(x_ref, g_ref, w_ref, lb_ref, q_ref, kvc_ref, kvs_ref, kvw_ref, kvh_ref,
                 gate_ref, hq_ref, hf_ref, hv_ref, hg_ref):
    xn = _rms(x_ref[...], g_ref[...]).astype(BF16)

    def seg(lo, n):
        return _dot(xn, w_ref[:, lo:lo + n])

    qa = seg(0, NSA_WIDTH) * SCALE
    for h in range(N_HEADS_NSA):
        q_ref[h] = qa[:, h * HEAD_DIM:(h + 1) * HEAD_DIM].astype(BF16)
    for br, ref in enumerate((kvc_ref, kvs_ref, kvw_ref)):
        kv = seg(_OFF_KV + br * KV_WIDTH, KV_WIDTH)
        ref[...] = kv
        for i in range(2 * N_KV_HEADS):
            kvh_ref[br * 2 * N_KV_HEADS + i] = kv[:, i * HEAD_DIM:(i + 1) * HEAD_DIM].astype(BF16)
    hq_ref[...] = _silu(seg(_OFF_HG, HG_WIDTH))
    lb = lb_ref[...]
    hf_ref[...] = lb + (1.0 - lb) * _sigmoid(seg(_OFF_HG + HG_WIDTH, HG_WIDTH))
    hv_ref[...] = seg(_OFF_HG + 2 * HG_WIDTH, HG_WIDTH)
    hg_ref[...] = _silu(seg(_OFF_HG + 3 * HG_WIDTH, HG_WIDTH))
    for k in range(N_KV_HEADS):
        gate_ref[k] = _sigmoid(seg(_OFF_GATE + k * LANES, LANES))


def _inproj(x, g, w_packed, lb, tm):
    m, d = x.shape
    assert m % tm == 0
    row = lambda i: (i, 0)
    const = lambda i: (0, 0)
    f32 = lambda n: jax.ShapeDtypeStruct((m, n), F32)
    out_shape = (
        jax.ShapeDtypeStruct((N_HEADS_NSA, m, HEAD_DIM), BF16),
        f32(KV_WIDTH), f32(KV_WIDTH), f32(KV_WIDTH),
        jax.ShapeDtypeStruct((N_BRANCH * 2 * N_KV_HEADS, m, HEAD_DIM), BF16),
        jax.ShapeDtypeStruct((N_KV_HEADS, m, LANES), F32),
        f32(HG_WIDTH), f32(HG_WIDTH), f32(HG_WIDTH), f32(HG_WIDTH),
    )
    blk = lambda n: pl.BlockSpec((tm, n), row)
    out_specs = (
        pl.BlockSpec((N_HEADS_NSA, tm, HEAD_DIM), lambda i: (0, i, 0)),
        blk(KV_WIDTH), blk(KV_WIDTH), blk(KV_WIDTH),
        pl.BlockSpec((N_BRANCH * 2 * N_KV_HEADS, tm, HEAD_DIM), lambda i: (0, i, 0)),
        pl.BlockSpec((N_KV_HEADS, tm, LANES), lambda i: (0, i, 0)),
        blk(HG_WIDTH), blk(HG_WIDTH), blk(HG_WIDTH), blk(HG_WIDTH),
    )
    return pl.pallas_call(
        _inproj_body,
        grid=(m // tm,),
        in_specs=[pl.BlockSpec((tm, d), row), pl.BlockSpec((1, d), const),
                  pl.BlockSpec((d, _W_PACKED), const), pl.BlockSpec((1, HG_WIDTH), const)],
        out_specs=out_specs, out_shape=out_shape,
        compiler_params=_cparams(("parallel",), V7X_VMEM_LIMIT),
        name="inproj",
    )(x, g.reshape(1, d), w_packed, lb.reshape(1, HG_WIDTH))


_HALF_W = N_KV_HEADS * HEAD_DIM
_CMP_K = CMP_STRIDE * _HALF_W
_PAGES_PER_STEP = 8


def _pack_cmp_weights(cmp_pos, cmp_w1, cmp_w2):
    hid = cmp_w1.shape[-1]
    w1h = cmp_w1.reshape(2, 2, CMP_STRIDE, HEAD_DIM, hid)
    eye = jnp.eye(N_KV_HEADS, dtype=cmp_w1.dtype)
    wbd = jnp.einsum('ctsdf,hg->cshdtgf', w1h, eye).reshape(2, _CMP_K, 2 * N_KV_HEADS * hid)
    posh = cmp_pos.reshape(2, 2, CMP_STRIDE, 1, HEAD_DIM)
    prow = jnp.broadcast_to(posh, (2, 2, CMP_STRIDE, N_KV_HEADS, HEAD_DIM)).reshape(2, 2, _CMP_K)
    prow = jnp.pad(prow, ((0, 0), (0, 6), (0, 0)))
    w2bd = jnp.einsum('cfd,hg->chfgd', cmp_w2, eye).reshape(2, N_KV_HEADS * hid, _HALF_W)
    return wbd.astype(BF16), prow.astype(BF16), w2bd.astype(BF16)


def _compress_body(pt_ref, *refs, n_pg, n_half, halves_per_page, hid2):
    del pt_ref
    page_refs = refs[:n_pg]
    wbd_ref, prow_ref, w2_ref, out_ref, a_ref = refs[n_pg:]
    g = pl.program_id(1)
    for k in range(n_pg):
        base = pl.multiple_of((g * n_pg + k) * halves_per_page, halves_per_page)
        for s in range(CMP_STRIDE):
            for c in range(2):
                a_ref[c, pl.ds(base, halves_per_page), s * _HALF_W:(s + 1) * _HALF_W] = (
                    page_refs[k][0, pl.ds(2 * s + c, halves_per_page, stride=2 * CMP_STRIDE), :])

    @pl.when(g == pl.num_programs(1) - 1)
    def _():
        col = lax.broadcasted_iota(jnp.int32, (1, 2 * hid2), 1)
        for c in range(2):
            z = _dot(a_ref[c].astype(BF16), wbd_ref[c])
            zp = _dot(prow_ref[c], wbd_ref[c])
            z = z + jnp.where(col < hid2, zp[0:1], zp[1:2])
            hi_next = pltpu.roll(z[:, hid2:], n_half - 1, axis=0)
            hidden = _silu(z[:, :hid2] + hi_next)
            out_ref[0, :, c * _HALF_W:(c + 1) * _HALF_W] = _dot(hidden.astype(BF16), w2_ref[c])


def _compress(rows_paged, page_table, wbd, prow, w2bd):
    n_pool, page, width = rows_paged.shape
    b, n_pages = page_table.shape
    assert page % CMP_STRIDE == 0 and width == KV_WIDTH
    n_pg = min(_PAGES_PER_STEP, n_pages)
    assert n_pages % n_pg == 0
    halves_per_page = page // CMP_STRIDE
    n_half = n_pages * halves_per_page
    hid2 = wbd.shape[-1] // 2

    rows_paged = rows_paged.reshape(n_pool, 2 * page, _HALF_W)

    def page_spec(k):
        return pl.BlockSpec((1, 2 * page, _HALF_W), lambda bi, gi, pt: (pt[bi, gi * n_pg + k], 0, 0))

    const3 = lambda bi, gi, pt: (0, 0, 0)
    grid_spec = pltpu.PrefetchScalarGridSpec(
        num_scalar_prefetch=1,
        grid=(b, n_pages // n_pg),
        in_specs=[page_spec(k) for k in range(n_pg)] + [
            pl.BlockSpec(wbd.shape, const3), pl.BlockSpec(prow.shape, const3),
            pl.BlockSpec(w2bd.shape, const3)],
        out_specs=pl.BlockSpec((1, n_half, width), lambda bi, gi, pt: (bi, 0, 0)),
        scratch_shapes=[pltpu.VMEM((2, n_half, _CMP_K), F32)],
    )
    body = functools.partial(_compress_body, n_pg=n_pg, n_half=n_half,
                             halves_per_page=halves_per_page, hid2=hid2)
    return pl.pallas_call(
        body, grid_spec=grid_spec,
        out_shape=jax.ShapeDtypeStruct((b, n_half, width), F32),
        compiler_params=_cparams(("parallel", "arbitrary"), V7X_VMEM_LIMIT),
        name="compress",
    )(page_table, *([rows_paged] * n_pg), wbd, prow, w2bd)


def _overlap_matrix(n_half, ns_pad, n_slc):
    cs = lax.broadcasted_iota(jnp.int32, (n_half, 1), 0) * CMP_STRIDE
    m = lax.broadcasted_iota(jnp.int32, (1, ns_pad), 1)
    ss = m * SEL_BLOCK
    return ((cs < ss + SEL_BLOCK) & (cs + CMP_BLOCK > ss) & (m < n_slc)).astype(BF16)


def _block_scores(imp, tpos, n_slc):
    m = lax.broadcasted_iota(jnp.int32, (1, imp.shape[-1]), 1)
    qblk = _fdiv(tpos, SEL_BLOCK)
    forced = (m == 0) | (m == qblk) | (m == qblk - 1)
    valid = (m * SEL_BLOCK <= tpos) & (m < n_slc)
    score = jnp.where(valid, jnp.where(forced, FORCED_SCORE, imp), NEG_INF)
    return jnp.where(m < n_slc, score, BELOW_ALL), valid


def _topk_rounds(score, n_rounds):
    lane = lax.broadcasted_iota(jnp.int32, (1, score.shape[-1]), 1).astype(F32)
    sc = score
    for r in range(n_rounds):
        mx = jnp.max(sc, axis=-1, keepdims=True)
        am = jnp.min(jnp.where(sc == mx, lane, 1e9), axis=-1, keepdims=True)
        hit = lane == am
        sc = jnp.where(hit, BELOW_ALL, sc)
        yield r, am, mx, hit


def _masked_softmax(s, mask):
    s = jnp.where(mask, s, NEG_INF)
    mx = jnp.max(s, axis=-1, keepdims=True)
    e = jnp.where(mask, jnp.exp(s - mx), 0.0)
    den = jnp.sum(e, axis=-1, keepdims=True)
    return e * jnp.where(den > 0.0, 1.0 / den, 0.0)


def _nsa_prompt_body(q_ref, gate_ref, kc_ref, vc_ref, ks_ref, vs_ref, kw_ref, vw_ref, o_ref,
                     m_ref, l_ref, acc_ref, *, tq, tk, n_half, n_slc, ns_pad):
    g4 = GQA_GROUP
    i = pl.program_id(2)
    t0 = i * tq
    q = q_ref[...].reshape(g4 * tq, HEAD_DIM)
    tpos = t0 + lax.broadcasted_iota(jnp.int32, (tq, 1), 0)

    kc = kc_ref[0, 0]
    vc = vc_ref[0, 0]
    s = _dot_nt(q, kc).reshape(g4, tq, n_half)
    j = lax.broadcasted_iota(jnp.int32, (1, n_half), 1)
    cmask = (j * CMP_STRIDE + (CMP_BLOCK - 1)) <= tpos
    p = _masked_softmax(s, cmask[None])
    o_cmp = _dot(p.reshape(g4 * tq, n_half).astype(BF16), vc).reshape(g4, tq, HEAD_DIM)

    psum = p[0]
    for g in range(1, g4):
        psum = psum + p[g]
    ov = _overlap_matrix(n_half, ns_pad, n_slc)
    ph, plo = _split_bf16(psum)
    imp = _dot(ph, ov) + _dot(plo, ov)
    score, valid = _block_scores(imp, tpos, n_slc)
    sel = jnp.zeros(score.shape, F32)
    for _, _, _, hit in _topk_rounds(score, min(SEL_TOPK, n_slc)):
        sel = jnp.where(hit, 1.0, sel)
    selb = jnp.where(valid, sel, 0.0).astype(BF16)

    def flash(k_ref, v_ref, n_lo, n_hi, mask_fn):
        m_ref[...] = jnp.full(m_ref.shape, NEG_INF, F32)
        l_ref[...] = jnp.zeros(l_ref.shape, F32)
        acc_ref[...] = jnp.zeros(acc_ref.shape, F32)

        def body(n, carry):
            k0 = pl.multiple_of(n * tk, tk)
            k = k_ref[0, pl.ds(k0, tk), :]
            v = v_ref[0, pl.ds(k0, tk), :]
            kpos = k0 + lax.broadcasted_iota(jnp.int32, (1, tk), 1)
            msk = mask_fn(kpos)[None]
            sc = jnp.where(msk, _dot_nt(q, k).reshape(g4, tq, tk), NEG_INF)
            m_old = m_ref[...]
            m_new = jnp.maximum(m_old, jnp.max(sc, axis=-1, keepdims=True))
            e = jnp.where(msk, jnp.exp(sc - m_new), 0.0)
            alpha = jnp.exp(m_old - m_new)
            l_ref[...] = alpha * l_ref[...] + jnp.sum(e, axis=-1, keepdims=True)
            pv = _dot(e.reshape(g4 * tq, tk).astype(BF16), v).reshape(g4, tq, HEAD_DIM)
            acc_ref[...] = alpha * acc_ref[...] + pv
            m_ref[...] = m_new
            return carry

        lax.fori_loop(n_lo, n_hi, body, 0)
        l = l_ref[...]
        return acc_ref[...] * jnp.where(l > 0.0, 1.0 / l, 0.0)

    def sel_mask(kpos):
        mrow = lax.broadcasted_iota(jnp.int32, (ns_pad, 1), 0)
        expand = (mrow == _fdiv(kpos, SEL_BLOCK)).astype(BF16)
        return (_dot(selb, expand) > 0.5) & (kpos <= tpos)

    def win_mask(kpos):
        dist = tpos - kpos
        return (dist >= 0) & (dist < WINDOW)

    n_hi = (t0 + tq + tk - 1) // tk
    o_slc = flash(ks_ref, vs_ref, 0, n_hi, sel_mask)
    o_win = flash(kw_ref, vw_ref, jnp.maximum(t0 - (WINDOW - 1), 0) // tk, n_hi, win_mask)

    gt = gate_ref[0]
    for g in range(g4):
        o = (gt[:, g:g + 1] * o_cmp[g] + gt[:, g4 + g:g4 + g + 1] * o_slc[g]
             + gt[:, 2 * g4 + g:2 * g4 + g + 1] * o_win[g])
        o_ref[:, g * HEAD_DIM:(g + 1) * HEAD_DIM] = o


def _nsa_prompt(q_heads, gate, kvc_blocks_heads, kv_heads, b, t, tq):
    m = b * t
    assert t % tq == 0
    tk = tq
    nt = t // tq
    n_half = kvc_blocks_heads.shape[2]
    n_slc = -(-t // SEL_BLOCK)
    ns_pad = _round_up(n_slc, LANES)
    nkv = N_KV_HEADS

    def kv_spec(branch, which):
        base = branch * 2 * nkv + which * nkv
        return pl.BlockSpec((1, t, HEAD_DIM), lambda bi, ki, i: (base + ki, bi, 0))

    body = functools.partial(_nsa_prompt_body, tq=tq, tk=tk, n_half=n_half, n_slc=n_slc, ns_pad=ns_pad)
    return pl.pallas_call(
        body,
        grid=(b, nkv, nt),
        in_specs=[
            pl.BlockSpec((GQA_GROUP, tq, HEAD_DIM), lambda bi, ki, i: (ki, bi * nt + i, 0)),
            pl.BlockSpec((1, tq, LANES), lambda bi, ki, i: (ki, bi * nt + i, 0)),
            pl.BlockSpec((1, 1, n_half, HEAD_DIM), lambda bi, ki, i: (bi, ki, 0, 0)),
            pl.BlockSpec((1, 1, n_half, HEAD_DIM), lambda bi, ki, i: (bi, nkv + ki, 0, 0)),
            kv_spec(1, 0), kv_spec(1, 1), kv_spec(2, 0), kv_spec(2, 1),
        ],
        out_specs=pl.BlockSpec((tq, GQA_GROUP * HEAD_DIM), lambda bi, ki, i: (bi * nt + i, ki)),
        out_shape=jax.ShapeDtypeStruct((m, NSA_WIDTH), F32),
        scratch_shapes=[pltpu.VMEM((GQA_GROUP, tq, 1), F32), pltpu.VMEM((GQA_GROUP, tq, 1), F32),
                        pltpu.VMEM((GQA_GROUP, tq, HEAD_DIM), F32)],
        compiler_params=_cparams(("parallel", "parallel", "arbitrary"), V7X_VMEM_LIMIT),
        name="nsa_prompt",
    )(q_heads, gate, kvc_blocks_heads, kvc_blocks_heads, kv_heads, kv_heads, kv_heads, kv_heads)


def _nsa_sample_a_body(q_ref, blk_ref, win_ref, new_ref, ocmp_ref, owin_ref, idx_ref, ok_ref,
                       *, qpos, n_half, n_slc, ns_pad, w_buf):
    q = q_ref[0].astype(BF16)
    blk = blk_ref[0]
    kc = blk[:, :_HALF_W].astype(BF16)
    vc = blk[:, _HALF_W:].astype(BF16)
    j = lax.broadcasted_iota(jnp.int32, (1, n_half), 1)
    cmask = (j * CMP_STRIDE + (CMP_BLOCK - 1)) <= qpos
    p = _masked_softmax(_dot_nt(q, kc), cmask)
    ocmp_ref[0] = _dot(p.astype(BF16), vc)

    nh = N_HEADS_NSA
    same = (_fdiv(lax.broadcasted_iota(jnp.int32, (nh, nh), 0), GQA_GROUP)
            == _fdiv(lax.broadcasted_iota(jnp.int32, (nh, nh), 1), GQA_GROUP)).astype(BF16)
    ph, plo = _split_bf16(p)
    psum = _dot(same, ph) + _dot(same, plo)
    ov = _overlap_matrix(n_half, ns_pad, n_slc)
    sh, slo = _split_bf16(psum)
    imp = _dot(sh, ov) + _dot(slo, ov)
    tpos = jnp.full((nh, 1), qpos, jnp.int32)
    score, _ = _block_scores(imp, tpos, n_slc)
    lane = lax.broadcasted_iota(jnp.int32, (1, LANES), 1)
    idx = jnp.zeros((nh, LANES), jnp.int32)
    ok = jnp.zeros((nh, LANES), jnp.int32)
    for r, am, mx, _ in _topk_rounds(score, min(SEL_TOPK, n_slc)):
        idx = jnp.where(lane == r, am.astype(jnp.int32), idx)
        ok = jnp.where(lane == r, (mx > 0.5 * NEG_INF).astype(jnp.int32), ok)
    idx_ref[0] = idx
    ok_ref[0] = ok

    st = win_ref[0]
    kw = st[:, :_HALF_W].astype(BF16)
    vw = st[:, _HALF_W:].astype(BF16)
    r = lax.broadcasted_iota(jnp.int32, (1, w_buf), 1)
    kpos = qpos - w_buf + r
    dist = qpos - kpos
    wm = (dist >= 0) & (dist < WINDOW) & (kpos >= 0)
    sw = jnp.where(wm, _dot_nt(q, kw), NEG_INF)
    new = new_ref[0]
    kn = new[:, :_HALF_W]
    vn = new[:, _HALF_W:]
    sn = jnp.sum(q.astype(F32) * kn, axis=-1, keepdims=True)
    mx = jnp.maximum(jnp.max(sw, axis=-1, keepdims=True), sn)
    ew = jnp.where(wm, jnp.exp(sw - mx), 0.0)
    en = jnp.exp(sn - mx)
    den = jnp.sum(ew, axis=-1, keepdims=True) + en
    owin_ref[0] = (_dot(ew.astype(BF16), vw) + en * vn) / den


def _nsa_sample_a(qexp, blocks, win_state, new_kvw, qpos):
    bs, n_half, _ = blocks.shape
    w_buf = win_state.shape[1]
    n_slc = -(-(qpos + 1) // SEL_BLOCK)
    ns_pad = _round_up(n_slc, LANES)
    nh = N_HEADS_NSA
    row3 = lambda bi: (bi, 0, 0)
    body = functools.partial(_nsa_sample_a_body, qpos=qpos, n_half=n_half, n_slc=n_slc,
                             ns_pad=ns_pad, w_buf=w_buf)
    o = jax.ShapeDtypeStruct((bs, nh, LANES), F32)
    oi = jax.ShapeDtypeStruct((bs, nh, LANES), jnp.int32)
    blk = pl.BlockSpec((1, nh, LANES), row3)
    return pl.pallas_call(
        body, grid=(bs,),
        in_specs=[blk, pl.BlockSpec((1, n_half, KV_WIDTH), row3),
                  pl.BlockSpec((1, w_buf, KV_WIDTH), row3), pl.BlockSpec((1, 1, KV_WIDTH), row3)],
        out_specs=(blk, blk, blk, blk), out_shape=(o, o, oi, oi),
        compiler_params=_cparams(("parallel",)),
        name="nsa_sample_cmp_win",
    )(qexp, blocks, win_state, new_kvw)


def _nsa_sample_b_body(idx_ref, ok_ref, pt_ref, q_ref, *refs, n_sel, qpos, n_past_blk):
    del pt_ref
    blk_refs = refs[:n_sel]
    new_ref, ocmp_ref, owin_ref, gate_ref, o_ref = refs[n_sel:]
    bi = pl.program_id(0)
    ki = pl.program_id(1)
    q = q_ref[0].astype(BF16)
    x = lax.broadcasted_iota(jnp.int32, (1, SEL_BLOCK), 1)
    scores, vals = [], []
    n_new = jnp.int32(0)
    for jx in range(n_sel):
        bidx = idx_ref[bi, ki, jx]
        good = ok_ref[bi, ki, jx] > 0
        blk = blk_refs[jx][0]
        limit = jnp.where(good & (bidx < n_past_blk), qpos, -1)
        msk = (bidx * SEL_BLOCK + x) <= limit
        scores.append((jnp.where(msk, _dot_nt(q, blk[:, :_HALF_W].astype(BF16)), NEG_INF), msk))
        vals.append(blk[:, _HALF_W:].astype(BF16))
        n_new = n_new + jnp.where(good & (bidx == n_past_blk), 1, 0)
    has_new = (n_new + jnp.zeros((1, 1), jnp.int32)) > 0
    new = new_ref[0]
    sn = jnp.where(has_new, jnp.sum(q.astype(F32) * new[:, :_HALF_W], axis=-1, keepdims=True), NEG_INF)
    mx = sn
    for sc, _ in scores:
        mx = jnp.maximum(mx, jnp.max(sc, axis=-1, keepdims=True))
    en = jnp.where(has_new, jnp.exp(sn - mx), 0.0)
    den = en
    acc = en * new[:, _HALF_W:]
    for (sc, msk), v in zip(scores, vals):
        e = jnp.where(msk, jnp.exp(sc - mx), 0.0)
        den = den + jnp.sum(e, axis=-1, keepdims=True)
        acc = acc + _dot(e.astype(BF16), v)
    o_slc = acc * jnp.where(den > 0.0, 1.0 / den, 0.0)
    gt = gate_ref[0]
    o_ref[0, 0] = gt[0] * ocmp_ref[0] + gt[1] * o_slc + gt[2] * owin_ref[0]


def _nsa_sample_b(top_idx, top_ok, page_table, qexp, cache_slc, new_kvs, o_cmp, o_win, gate_exp, qpos):
    bs = qexp.shape[0]
    n_pool, page, width = cache_slc.shape
    per_page = page // SEL_BLOCK
    assert page % SEL_BLOCK == 0
    n_past_blk = page_table.shape[1] * per_page
    n_sel = top_idx.shape[-1]
    nh = N_HEADS_NSA
    halves = cache_slc.reshape(n_pool * per_page, SEL_BLOCK, width)

    def sel_spec(jx):
        def imap(bi, ki, idx, ok, pt):
            blk = jnp.minimum(idx[bi, ki, jx], n_past_blk - 1)
            return (pt[bi, blk // per_page] * per_page + blk % per_page, 0, 0)
        return pl.BlockSpec((1, SEL_BLOCK, width), imap)

    row3 = lambda bi, ki, idx, ok, pt: (bi, 0, 0)
    blk = pl.BlockSpec((1, nh, LANES), row3)
    grid_spec = pltpu.PrefetchScalarGridSpec(
        num_scalar_prefetch=3, grid=(bs, N_KV_HEADS),
        in_specs=[blk] + [sel_spec(jx) for jx in range(n_sel)] + [
            pl.BlockSpec((1, 1, width), row3), blk, blk,
            pl.BlockSpec((1, N_BRANCH, nh, LANES), lambda bi, ki, idx, ok, pt: (bi, 0, 0, 0))],
        out_specs=pl.BlockSpec((1, 1, nh, LANES), lambda bi, ki, idx, ok, pt: (bi, ki, 0, 0)),
    )
    body = functools.partial(_nsa_sample_b_body, n_sel=n_sel, qpos=qpos, n_past_blk=n_past_blk)
    return pl.pallas_call(
        body, grid_spec=grid_spec,
        out_shape=jax.ShapeDtypeStruct((bs, N_KV_HEADS, nh, LANES), F32),
        compiler_params=_cparams(("parallel", "parallel")),
        name="nsa_sample_sel",
    )(top_idx, top_ok, page_table, qexp, *([halves] * n_sel), new_kvs, o_cmp, o_win, gate_exp)


def _hgrn_prompt_body(q_ref, f_ref, v_ref, g_ref, nw_ref, o_ref, s_ref, st_ref, *, tc):
    c = HG_CHUNK
    i = pl.program_id(2)

    @pl.when(i == 0)
    def _():
        st_ref[...] = jnp.zeros(st_ref.shape, F32)

    f = f_ref[...]
    lf = jnp.log(f)
    k = 1.0 - f
    v = v_ref[...]
    r = lax.broadcasted_iota(jnp.int32, (tc, HG_DK), 0) & (c - 1)
    cum = lf
    sh = 1
    while sh < c:
        cum = cum + jnp.where(r >= sh, pltpu.roll(cum, sh, axis=0), 0.0)
        sh *= 2
    q_dec = (q_ref[...] * jnp.exp(cum)).astype(BF16)
    k_inv = (k * jnp.exp(-cum)).astype(BF16)
    causal = (lax.broadcasted_iota(jnp.int32, (c, c), 0) >= lax.broadcasted_iota(jnp.int32, (c, c), 1))
    nw = nw_ref[...]
    st = st_ref[...]
    for ci in range(tc // c):
        sl = slice(ci * c, (ci + 1) * c)
        cum_c = cum[sl]
        last = cum_c[c - 1:c, :]
        vc = v[sl].astype(BF16)
        k_dec = (k[sl] * jnp.exp(last - cum_c)).astype(BF16)
        attn = jnp.where(causal, _dot_nt(q_dec[sl], k_inv[sl]), 0.0)
        o = _dot_nt(q_dec[sl], st.astype(BF16)) + _dot(attn.astype(BF16), vc)
        st = st * jnp.exp(last) + _dot_tn(vc, k_dec)
        o_ref[sl, :] = (_rms(o, nw) * g_ref[sl, :]).astype(o_ref.dtype)
    st_ref[...] = st

    @pl.when(i == pl.num_programs(2) - 1)
    def _():
        s_ref[0, 0] = st.T


def _hgrn_prompt(hq, hf, hv, hg, norm_w, b, t, tc):
    m = b * t
    assert t % tc == 0 and tc % HG_CHUNK == 0
    nt = t // tc
    blk = pl.BlockSpec((tc, HG_DK), lambda bi, hi, i: (bi * nt + i, hi))
    return pl.pallas_call(
        functools.partial(_hgrn_prompt_body, tc=tc),
        grid=(b, HG_HEADS, nt),
        in_specs=[blk, blk, blk, blk, pl.BlockSpec((1, HG_DV), lambda bi, hi, i: (0, hi))],
        out_specs=(blk, pl.BlockSpec((1, 1, HG_DK, HG_DV), lambda bi, hi, i: (bi, hi, 0, 0))),
        out_shape=(jax.ShapeDtypeStruct((m, HG_WIDTH), F32),
                   jax.ShapeDtypeStruct((b, HG_HEADS, HG_DK, HG_DV), F32)),
        scratch_shapes=[pltpu.VMEM((HG_DV, HG_DK), F32)],
        compiler_params=_cparams(("parallel", "parallel", "arbitrary")),
        name="hgrn_prompt",
    )(hq, hf, hv, hg, norm_w.reshape(1, HG_WIDTH))


def _hgrn_sample_body(q_ref, f_ref, v_ref, g_ref, nw_ref, s0_ref, o_ref, s_ref):
    def column(x):
        return jnp.broadcast_to(x, (HG_DV, HG_DK)).T

    for h in range(HG_HEADS):
        sl = slice(h * HG_DK, (h + 1) * HG_DK)
        q = q_ref[0, :, sl]
        f = f_ref[0, :, sl]
        v = v_ref[0, :, sl]
        k = 1.0 - f
        s0 = s0_ref[0, h]
        qk = jnp.sum(q * k, axis=-1, keepdims=True)
        o = jnp.sum(column(q * f) * s0, axis=0, keepdims=True) + qk * v
        s_ref[0, h] = column(f) * s0 + column(k) * v
        o_ref[0, :, sl] = _rms(o, nw_ref[:, sl]) * g_ref[0, :, sl]


def _hgrn_sample(hq, hf, hv, hg, norm_w, s0):
    bs = hq.shape[0]
    r3 = lambda x: x.reshape(bs, 1, HG_WIDTH)
    row = pl.BlockSpec((1, 1, HG_WIDTH), lambda bi: (bi, 0, 0))
    st = pl.BlockSpec((1, HG_HEADS, HG_DK, HG_DV), lambda bi: (bi, 0, 0, 0))
    o, s = pl.pallas_call(
        _hgrn_sample_body, grid=(bs,),
        in_specs=[row, row, row, row, pl.BlockSpec((1, HG_WIDTH), lambda bi: (0, 0)), st],
        out_specs=(row, st),
        out_shape=(jax.ShapeDtypeStruct((bs, 1, HG_WIDTH), F32),
                   jax.ShapeDtypeStruct(s0.shape, F32)),
        compiler_params=_cparams(("parallel",)),
        name="hgrn_sample",
    )(r3(hq), r3(hf), r3(hv), r3(hg), norm_w.reshape(1, HG_WIDTH), s0)
    return o.reshape(bs, HG_WIDTH), s


def _tail_body(h_ref, on_ref, oh_ref, p_ref, gn_ref, wo_ref, gf_ref, wgu_ref, wd_ref, gp_ref,
               wpg_ref, wpp_ref, gl_ref, y_ref, *, d_ff, ff_chunk):
    a = _rms(on_ref[...], gn_ref[...]).astype(BF16)
    b = oh_ref[...].astype(BF16)
    nsa_w = on_ref.shape[-1]
    h = h_ref[...] + _dot(a, wo_ref[:nsa_w, :]) + _dot(b, wo_ref[nsa_w:, :])
    x = _rms(h, gf_ref[...]).astype(BF16)
    for c0 in range(0, d_ff, ff_chunk):
        gate = _dot(x, wgu_ref[:, c0:c0 + ff_chunk])
        up = _dot(x, wgu_ref[:, d_ff + c0:d_ff + c0 + ff_chunk])
        h = h + _dot((_silu(gate) * up).astype(BF16), wd_ref[c0:c0 + ff_chunk, :])
    x = _rms(h, gp_ref[...]).astype(BF16)
    h = h + _sigmoid(_dot(x, wpg_ref[...])) * _dot(p_ref[...].astype(BF16), wpp_ref[...])
    y_ref[...] = _rms(h, gl_ref[...])


def _tail(h, o_nsa, o_hg, p_emb, norm_nsa, w_out, norm_ffn, w_gate_up, w_down, norm_ple, w_ple_gate,
          w_ple_proj, norm_final, tm):
    m, d = h.shape
    assert m % tm == 0
    d_ff = w_down.shape[0]
    ff_chunk = d_ff
    row = lambda i: (i, 0)
    const = lambda i: (0, 0)
    vec = lambda g: g.reshape(1, -1)
    full = lambda w: pl.BlockSpec(w.shape, const)
    blk = lambda x: pl.BlockSpec((tm, x.shape[1]), row)
    args = (h, o_nsa, o_hg, p_emb, vec(norm_nsa), w_out, vec(norm_ffn), w_gate_up, w_down,
            vec(norm_ple), w_ple_gate, w_ple_proj, vec(norm_final))
    in_specs = [blk(h), blk(o_nsa), blk(o_hg), blk(p_emb)] + [full(w) for w in args[4:]]
    return pl.pallas_call(
        functools.partial(_tail_body, d_ff=d_ff, ff_chunk=ff_chunk),
        grid=(m // tm,), in_specs=in_specs, out_specs=pl.BlockSpec((tm, d), row),
        out_shape=jax.ShapeDtypeStruct((m, d), F32),
        compiler_params=_cparams(("parallel",), V7X_VMEM_LIMIT),
        name="tail",
    )(*args)


def _tile(m, pref):
    return pref if m % pref == 0 else m


def kernel(x_prompt, x_sample, cache_kv_cmp, cache_kv_slc, state_kv_win, state_hgrn, page_table,
           p_prompt, p_sample, norm_mix, w_in, cmp_pos, cmp_w1, cmp_w2, hg_lb, norm_nsa_out,
           norm_hg_out, w_out, norm_ffn, w_gate_up, w_down, norm_ple, w_ple_gate, w_ple_proj,
           norm_final):
    depth = w_in.shape[0]
    assert depth == 1
    l = 0
    b, t, d = x_prompt.shape
    bs, ts, _ = x_sample.shape
    assert ts == 1
    n_pool, page = cache_kv_cmp.shape[1:3]
    n_pages = page_table.shape[1]
    past = n_pages * page
    w_buf = state_kv_win.shape[2]
    nkv = N_KV_HEADS

    lb = jnp.cumsum(jax.nn.softmax(hg_lb.astype(F32), axis=0), axis=0)[l]
    w_packed = _pack_w_in(w_in[l])
    wbd, prow, w2bd = _pack_cmp_weights(cmp_pos[l], cmp_w1[l], cmp_w2[l])
    tail_w = (norm_nsa_out[l], w_out[l].astype(BF16), norm_ffn[l], w_gate_up[l].astype(BF16),
              w_down[l].astype(BF16), norm_ple[l], w_ple_gate[l].astype(BF16),
              w_ple_proj[l].astype(BF16), norm_final)

    m = b * t
    xp = x_prompt.reshape(m, d)
    (q_h, kvc, kvs, kvw, kv_h, gate, hq, hf, hv, hg) = _inproj(xp, norm_mix[l], w_packed, lb, _tile(m, 256))
    pages_p = t // page
    pt_p = (jnp.arange(b, dtype=jnp.int32)[:, None] * pages_p + jnp.arange(pages_p, dtype=jnp.int32)[None, :])
    blocks_p = _compress(kvc.reshape(m // page, page, KV_WIDTH), pt_p, wbd, prow, w2bd)
    n_half_p = blocks_p.shape[1]
    blocks_ph = blocks_p.reshape(b, n_half_p, 2 * nkv, HEAD_DIM).transpose(0, 2, 1, 3).astype(BF16)
    o_nsa_p = _nsa_prompt(q_h, gate, blocks_ph, kv_h, b, t, _tile(t, 128))
    o_hg_p, s_p = _hgrn_prompt(hq, hf, hv, hg, norm_hg_out[l], b, t, _tile(t, 256))
    y_p = _tail(xp, o_nsa_p, o_hg_p, p_prompt[l].reshape(m, -1), *tail_w, tm=_tile(m, 256))

    xs = x_sample.reshape(bs, d)
    (q_hs, kvc_s, kvs_s, kvw_s, _, gate_s, hq_s, hf_s, hv_s, hg_s) = _inproj(
        xs, norm_mix[l], w_packed, lb, bs)
    blocks_s = _compress(cache_kv_cmp[l].reshape(n_pool, page, KV_WIDTH), page_table, wbd, prow, w2bd)
    q_rows = q_hs.astype(F32).transpose(1, 0, 2)
    lane_half = (jnp.arange(N_HEADS_NSA) // GQA_GROUP)[:, None] == jnp.arange(nkv)[None, :]
    qexp = (q_rows[:, :, None, :] * lane_half[None, :, :, None]).reshape(bs, N_HEADS_NSA, LANES)
    o_cmp_s, o_win_s, idx_s, ok_s = _nsa_sample_a(
        qexp, blocks_s, state_kv_win[l].reshape(bs, w_buf, KV_WIDTH), kvw_s.reshape(bs, 1, KV_WIDTH), past)
    n_sel = min(SEL_TOPK, -(-(past + 1) // SEL_BLOCK))
    top_idx = idx_s[:, ::GQA_GROUP, :n_sel]
    top_ok = ok_s[:, ::GQA_GROUP, :n_sel]
    gsm = gate_s[:, :, :N_BRANCH * GQA_GROUP].reshape(nkv, bs, N_BRANCH, GQA_GROUP)
    gate_exp = jnp.broadcast_to(gsm.transpose(1, 2, 0, 3).reshape(bs, N_BRANCH, N_HEADS_NSA, 1),
                                (bs, N_BRANCH, N_HEADS_NSA, LANES))
    o_sel = _nsa_sample_b(top_idx, top_ok, page_table, qexp,
                          cache_kv_slc[l].reshape(n_pool, page, KV_WIDTH),
                          kvs_s.reshape(bs, 1, KV_WIDTH), o_cmp_s, o_win_s, gate_exp, past)
    o5 = o_sel.reshape(bs, nkv, nkv, GQA_GROUP, nkv, HEAD_DIM)
    o_nsa_s = jnp.stack([o5[:, k, k, :, k, :] for k in range(nkv)], axis=1).reshape(bs, NSA_WIDTH)
    o_hg_s, s_s = _hgrn_sample(hq_s, hf_s, hv_s, hg_s, norm_hg_out[l], state_hgrn[l])
    y_s = _tail(xs, o_nsa_s, o_hg_s, p_sample[l].reshape(bs, -1), *tail_w, tm=bs)

    kv6 = lambda a, n, s: a.reshape(1, n, s, 2, nkv, HEAD_DIM)
    win_p = kvw.reshape(b, t, KV_WIDTH)[:, t - min(WINDOW, t):]
    win_s = jnp.concatenate([state_kv_win[l].reshape(bs, w_buf, KV_WIDTH), kvw_s.reshape(bs, 1, KV_WIDTH)],
                            axis=1)[:, -w_buf:]
    return (y_p.reshape(b, t, d), y_s.reshape(bs, ts, d),
            kv6(kvc, b, t), kv6(kvs, b, t), kv6(win_p, b, min(WINDOW, t)), s_p[None],
            kv6(kvc_s, bs, 1), kv6(kvs_s, bs, 1), kv6(win_s, bs, w_buf), s_s[None])
```

```python
import functools

import jax
import jax.numpy as jnp
from jax import lax
from jax.experimental import pallas as pl
from jax.experimental.pallas import tpu as pltpu

F32 = jnp.float32
BF16 = jnp.bfloat16

N_HEADS_NSA = 8
HEAD_DIM = 64
N_KV_HEADS = 2
GQA_GROUP = N_HEADS_NSA // N_KV_HEADS
CMP_BLOCK = 32
CMP_STRIDE = 16
SEL_BLOCK = 64
SEL_TOPK = 16
WINDOW = 512
N_BRANCH = 3
HG_HEADS = 4
HG_DK = 128
HG_DV = 128
HG_CHUNK = 32
SCALE = HEAD_DIM ** -0.5
NEG_INF = -1e30
FORCED_SCORE = 1e9
EPS = 1e-6
BELOW_ALL = -3.0e38

NSA_WIDTH = N_HEADS_NSA * HEAD_DIM
KV_WIDTH = 2 * N_KV_HEADS * HEAD_DIM
HG_WIDTH = HG_HEADS * HG_DK
LANES = 128
V7X_VMEM_LIMIT = 56 * 1024 * 1024


def _cparams(sem, vmem=None):
    return pltpu.CompilerParams(dimension_semantics=sem, vmem_limit_bytes=vmem)


def _sigmoid(x):
    return 1.0 / (1.0 + jnp.exp(-x))


def _silu(x):
    return x * _sigmoid(x)


def _rms(x, g):
    return x * lax.rsqrt(jnp.mean(x * x, axis=-1, keepdims=True) + EPS) * g


def _dot(a, b):
    return jnp.dot(a, b, preferred_element_type=F32)


def _dot_nt(a, b):
    return lax.dot_general(a, b, (((1,), (1,)), ((), ())), preferred_element_type=F32)


def _dot_tn(a, b):
    return lax.dot_general(a, b, (((0,), (0,)), ((), ())), preferred_element_type=F32)


def _split_bf16(x):
    hi = x.astype(BF16)
    lo = (x - hi.astype(F32)).astype(BF16)
    return hi, lo


def _round_up(x, m):
    return -(-x // m) * m


def _fdiv(x, n):
    assert n & (n - 1) == 0
    return jnp.right_shift(x, n.bit_length() - 1)


_OFF_KV = NSA_WIDTH
_OFF_HG = NSA_WIDTH + N_BRANCH * KV_WIDTH
_OFF_GATE = _OFF_HG + 4 * HG_WIDTH
_W_PACKED = _OFF_GATE + N_KV_HEADS * LANES


def _pack_w_in(w_in):
    d = w_in.shape[0]
    a = w_in[:, :_OFF_HG]
    gate = w_in[:, _OFF_HG:_OFF_HG + N_BRANCH * N_HEADS_NSA]
    b = w_in[:, _OFF_HG + N_BRANCH * N_HEADS_NSA:]
    gate = gate.reshape(d, N_BRANCH, N_KV_HEADS, GQA_GROUP).transpose(0, 2, 1, 3)
    gate = gate.reshape(d, N_KV_HEADS, N_BRANCH * GQA_GROUP)
    gate = jnp.pad(gate, ((0, 0), (0, 0), (0, LANES - N_BRANCH * GQA_GROUP)))
    return jnp.concatenate([a, b, gate.reshape(d, N_KV_HEADS * LANES)], axis=1).astype(BF16)


def _inproj_body(x_ref, g_ref, w_ref, lb_ref, q_ref, kv_ref, kvh_ref,
                 gate_ref, hq_ref, hf_ref, hv_ref, hg_ref, *, kv_transposed):
    xn = _rms(x_ref[...], g_ref[...]).astype(BF16)

    def seg(lo, n):
        return _dot(xn, w_ref[:, lo:lo + n])

    qa = seg(0, NSA_WIDTH) * SCALE
    for h in range(N_HEADS_NSA):
        q_ref[h] = qa[:, h * HEAD_DIM:(h + 1) * HEAD_DIM].astype(BF16)
    for br in range(N_BRANCH):
        kv = seg(_OFF_KV + br * KV_WIDTH, KV_WIDTH)
        if kv_transposed:
            kv_ref[br, 0] = kv.T.reshape(2 * N_KV_HEADS, HEAD_DIM, kv.shape[0])
        else:
            kv_ref[br] = kv
        for i in range(2 * N_KV_HEADS):
            kvh_ref[br * 2 * N_KV_HEADS + i] = kv[:, i * HEAD_DIM:(i + 1) * HEAD_DIM].astype(BF16)
    hq_ref[...] = _silu(seg(_OFF_HG, HG_WIDTH))
    lb = lb_ref[...]
    hf_ref[...] = lb + (1.0 - lb) * _sigmoid(seg(_OFF_HG + HG_WIDTH, HG_WIDTH))
    hv_ref[...] = seg(_OFF_HG + 2 * HG_WIDTH, HG_WIDTH)
    hg_ref[...] = _silu(seg(_OFF_HG + 3 * HG_WIDTH, HG_WIDTH))
    for k in range(N_KV_HEADS):
        gate_ref[k] = _sigmoid(seg(_OFF_GATE + k * LANES, LANES))


def _inproj(x, g, w_packed, lb, tm, seq=None):
    m, d = x.shape
    assert m % tm == 0
    if seq is None:
        kv_shape = jax.ShapeDtypeStruct((N_BRANCH, m, KV_WIDTH), F32)
        kv_spec = pl.BlockSpec((N_BRANCH, tm, KV_WIDTH), lambda i: (0, i, 0))
    else:
        b, t = seq
        assert b * t == m and t % tm == 0 and tm % LANES == 0
        nt = t // tm
        kv_shape = jax.ShapeDtypeStruct((N_BRANCH, b, 2 * N_KV_HEADS, HEAD_DIM, t), F32)
        kv_spec = pl.BlockSpec((N_BRANCH, 1, 2 * N_KV_HEADS, HEAD_DIM, tm),
                               lambda i: (0, i // nt, 0, 0, i % nt))
    row = lambda i: (i, 0)
    const = lambda i: (0, 0)
    f32 = lambda n: jax.ShapeDtypeStruct((m, n), F32)
    out_shape = (
        jax.ShapeDtypeStruct((N_HEADS_NSA, m, HEAD_DIM), BF16),
        kv_shape,
        jax.ShapeDtypeStruct((N_BRANCH * 2 * N_KV_HEADS, m, HEAD_DIM), BF16),
        jax.ShapeDtypeStruct((N_KV_HEADS, m, LANES), F32),
        f32(HG_WIDTH), f32(HG_WIDTH), f32(HG_WIDTH), f32(HG_WIDTH),
    )
    blk = lambda n: pl.BlockSpec((tm, n), row)
    out_specs = (
        pl.BlockSpec((N_HEADS_NSA, tm, HEAD_DIM), lambda i: (0, i, 0)),
        kv_spec,
        pl.BlockSpec((N_BRANCH * 2 * N_KV_HEADS, tm, HEAD_DIM), lambda i: (0, i, 0)),
        pl.BlockSpec((N_KV_HEADS, tm, LANES), lambda i: (0, i, 0)),
        blk(HG_WIDTH), blk(HG_WIDTH), blk(HG_WIDTH), blk(HG_WIDTH),
    )
    return pl.pallas_call(
        functools.partial(_inproj_body, kv_transposed=seq is not None),
        grid=(m // tm,),
        in_specs=[pl.BlockSpec((tm, d), row), pl.BlockSpec((1, d), const),
                  pl.BlockSpec((d, _W_PACKED), const), pl.BlockSpec((1, HG_WIDTH), const)],
        out_specs=out_specs, out_shape=out_shape,
        compiler_params=_cparams(("parallel",), V7X_VMEM_LIMIT),
        name="inproj",
    )(x, g.reshape(1, d), w_packed, lb.reshape(1, HG_WIDTH))


_HALF_W = N_KV_HEADS * HEAD_DIM
_CMP_K = CMP_STRIDE * _HALF_W
_PAGES_PER_STEP = 8


def _pack_cmp_weights(cmp_pos, cmp_w1, cmp_w2):
    hid = cmp_w1.shape[-1]
    w1h = cmp_w1.reshape(2, 2, CMP_STRIDE, HEAD_DIM, hid)
    eye = jnp.eye(N_KV_HEADS, dtype=cmp_w1.dtype)
    wbd = jnp.einsum('ctsdf,hg->cshdtgf', w1h, eye).reshape(2, _CMP_K, 2 * N_KV_HEADS * hid)
    posh = cmp_pos.reshape(2, 2, CMP_STRIDE, 1, HEAD_DIM)
    prow = jnp.broadcast_to(posh, (2, 2, CMP_STRIDE, N_KV_HEADS, HEAD_DIM)).reshape(2, 2, _CMP_K)
    prow = jnp.pad(prow, ((0, 0), (0, 6), (0, 0)))
    w2bd = jnp.einsum('cfd,hg->chfgd', cmp_w2, eye).reshape(2, N_KV_HEADS * hid, _HALF_W)
    return wbd.astype(BF16), prow.astype(BF16), w2bd.astype(BF16)


def _compress_body(pt_ref, *refs, n_pg, n_half, halves_per_page, hid2):
    del pt_ref
    page_refs = refs[:n_pg]
    wbd_ref, prow_ref, w2_ref, out_ref, a_ref, t_ref = refs[n_pg:]
    g = pl.program_id(1)
    page = halves_per_page * CMP_STRIDE
    for k in range(n_pg):
        base = pl.multiple_of((g * n_pg + k) * halves_per_page, halves_per_page)
        for c in range(2):
            t_ref[k, c] = page_refs[k][0, c].reshape(_HALF_W, page).T
            for s in range(CMP_STRIDE):
                a_ref[c, pl.ds(base, halves_per_page), s * _HALF_W:(s + 1) * _HALF_W] = (
                    t_ref[k, c, pl.ds(s, halves_per_page, stride=CMP_STRIDE), :])

    @pl.when(g == pl.num_programs(1) - 1)
    def _():
        col = lax.broadcasted_iota(jnp.int32, (1, 2 * hid2), 1)
        for c in range(2):
            z = _dot(a_ref[c].astype(BF16), wbd_ref[c])
            zp = _dot(prow_ref[c], wbd_ref[c])
            z = z + jnp.where(col < hid2, zp[0:1], zp[1:2])
            hi_next = pltpu.roll(z[:, hid2:], n_half - 1, axis=0)
            hidden = _silu(z[:, :hid2] + hi_next)
            out_ref[0, :, c * _HALF_W:(c + 1) * _HALF_W] = _dot(hidden.astype(BF16), w2_ref[c])


def _compress(pages_t, page_table, page_index, page, wbd, prow, w2bd):
    b, n_pages = page_table.shape
    width = KV_WIDTH
    assert page % CMP_STRIDE == 0 and page % LANES == 0
    n_pg = min(_PAGES_PER_STEP, n_pages)
    assert n_pages % n_pg == 0
    halves_per_page = page // CMP_STRIDE
    n_half = n_pages * halves_per_page
    hid2 = wbd.shape[-1] // 2

    def page_spec(k):
        return pl.BlockSpec((1, 2, N_KV_HEADS, HEAD_DIM, page),
                            lambda bi, gi, pt: page_index(pt, bi, gi * n_pg + k))

    const3 = lambda bi, gi, pt: (0, 0, 0)
    grid_spec = pltpu.PrefetchScalarGridSpec(
        num_scalar_prefetch=1,
        grid=(b, n_pages // n_pg),
        in_specs=[page_spec(k) for k in range(n_pg)] + [
            pl.BlockSpec(wbd.shape, const3), pl.BlockSpec(prow.shape, const3),
            pl.BlockSpec(w2bd.shape, const3)],
        out_specs=pl.BlockSpec((1, n_half, width), lambda bi, gi, pt: (bi, 0, 0)),
        scratch_shapes=[pltpu.VMEM((2, n_half, _CMP_K), F32), pltpu.VMEM((n_pg, 2, page, _HALF_W), F32)],
    )
    body = functools.partial(_compress_body, n_pg=n_pg, n_half=n_half,
                             halves_per_page=halves_per_page, hid2=hid2)
    return pl.pallas_call(
        body, grid_spec=grid_spec,
        out_shape=jax.ShapeDtypeStruct((b, n_half, width), F32),
        compiler_params=_cparams(("parallel", "arbitrary"), V7X_VMEM_LIMIT),
        name="compress",
    )(page_table, *([pages_t] * n_pg), wbd, prow, w2bd)


def _overlap_matrix(n_half, ns_pad, n_slc):
    cs = lax.broadcasted_iota(jnp.int32, (n_half, 1), 0) * CMP_STRIDE
    m = lax.broadcasted_iota(jnp.int32, (1, ns_pad), 1)
    ss = m * SEL_BLOCK
    return ((cs < ss + SEL_BLOCK) & (cs + CMP_BLOCK > ss) & (m < n_slc)).astype(BF16)


def _block_scores(imp, tpos, n_slc):
    m = lax.broadcasted_iota(jnp.int32, (1, imp.shape[-1]), 1)
    qblk = _fdiv(tpos, SEL_BLOCK)
    forced = (m == 0) | (m == qblk) | (m == qblk - 1)
    valid = (m * SEL_BLOCK <= tpos) & (m < n_slc)
    score = jnp.where(valid, jnp.where(forced, FORCED_SCORE, imp), NEG_INF)
    return jnp.where(m < n_slc, score, BELOW_ALL), valid


def _topk_rounds(score, n_rounds):
    lane = lax.broadcasted_iota(jnp.int32, (1, score.shape[-1]), 1).astype(F32)
    sc = score
    for r in range(n_rounds):
        mx = jnp.max(sc, axis=-1, keepdims=True)
        am = jnp.min(jnp.where(sc == mx, lane, 1e9), axis=-1, keepdims=True)
        hit = lane == am
        sc = jnp.where(hit, BELOW_ALL, sc)
        yield r, am, mx, hit


def _masked_softmax(s, mask):
    s = jnp.where(mask, s, NEG_INF)
    mx = jnp.max(s, axis=-1, keepdims=True)
    e = jnp.where(mask, jnp.exp(s - mx), 0.0)
    den = jnp.sum(e, axis=-1, keepdims=True)
    return e * jnp.where(den > 0.0, 1.0 / den, 0.0)


def _nsa_prompt_body(q_ref, gate_ref, kc_ref, vc_ref, ks_ref, vs_ref, kw_ref, vw_ref, o_ref,
                     m_ref, l_ref, acc_ref, *, tq, tk, n_half, n_slc, ns_pad):
    g4 = GQA_GROUP
    i = pl.program_id(2)
    t0 = i * tq
    q = q_ref[...].reshape(g4 * tq, HEAD_DIM)
    tpos = t0 + lax.broadcasted_iota(jnp.int32, (tq, 1), 0)

    kc = kc_ref[0, 0]
    vc = vc_ref[0, 0]
    s = _dot_nt(q, kc).reshape(g4, tq, n_half)
    j = lax.broadcasted_iota(jnp.int32, (1, n_half), 1)
    cmask = (j * CMP_STRIDE + (CMP_BLOCK - 1)) <= tpos
    p = _masked_softmax(s, cmask[None])
    o_cmp = _dot(p.reshape(g4 * tq, n_half).astype(BF16), vc).reshape(g4, tq, HEAD_DIM)

    psum = p[0]
    for g in range(1, g4):
        psum = psum + p[g]
    ov = _overlap_matrix(n_half, ns_pad, n_slc)
    ph, plo = _split_bf16(psum)
    imp = _dot(ph, ov) + _dot(plo, ov)
    score, valid = _block_scores(imp, tpos, n_slc)
    sel = jnp.zeros(score.shape, F32)
    for _, _, _, hit in _topk_rounds(score, min(SEL_TOPK, n_slc)):
        sel = jnp.where(hit, 1.0, sel)
    selb = jnp.where(valid, sel, 0.0).astype(BF16)

    def flash(k_ref, v_ref, n_lo, n_hi, mask_fn):
        m_ref[...] = jnp.full(m_ref.shape, NEG_INF, F32)
        l_ref[...] = jnp.zeros(l_ref.shape, F32)
        acc_ref[...] = jnp.zeros(acc_ref.shape, F32)

        def body(n, carry):
            k0 = pl.multiple_of(n * tk, tk)
            k = k_ref[0, pl.ds(k0, tk), :]
            v = v_ref[0, pl.ds(k0, tk), :]
            kpos = k0 + lax.broadcasted_iota(jnp.int32, (1, tk), 1)
            msk = mask_fn(kpos)[None]
            sc = jnp.where(msk, _dot_nt(q, k).reshape(g4, tq, tk), NEG_INF)
            m_old = m_ref[...]
            m_new = jnp.maximum(m_old, jnp.max(sc, axis=-1, keepdims=True))
            e = jnp.where(msk, jnp.exp(sc - m_new), 0.0)
            alpha = jnp.exp(m_old - m_new)
            l_ref[...] = alpha * l_ref[...] + jnp.sum(e, axis=-1, keepdims=True)
            pv = _dot(e.reshape(g4 * tq, tk).astype(BF16), v).reshape(g4, tq, HEAD_DIM)
            acc_ref[...] = alpha * acc_ref[...] + pv
            m_ref[...] = m_new
            return carry

        lax.fori_loop(n_lo, n_hi, body, 0)
        l = l_ref[...]
        return acc_ref[...] * jnp.where(l > 0.0, 1.0 / l, 0.0)

    def sel_mask(kpos):
        mrow = lax.broadcasted_iota(jnp.int32, (ns_pad, 1), 0)
        expand = (mrow == _fdiv(kpos, SEL_BLOCK)).astype(BF16)
        return (_dot(selb, expand) > 0.5) & (kpos <= tpos)

    def win_mask(kpos):
        dist = tpos - kpos
        return (dist >= 0) & (dist < WINDOW)

    n_hi = (t0 + tq + tk - 1) // tk
    o_slc = flash(ks_ref, vs_ref, 0, n_hi, sel_mask)
    o_win = flash(kw_ref, vw_ref, jnp.maximum(t0 - (WINDOW - 1), 0) // tk, n_hi, win_mask)

    gt = gate_ref[0]
    for g in range(g4):
        o = (gt[:, g:g + 1] * o_cmp[g] + gt[:, g4 + g:g4 + g + 1] * o_slc[g]
             + gt[:, 2 * g4 + g:2 * g4 + g + 1] * o_win[g])
        o_ref[:, g * HEAD_DIM:(g + 1) * HEAD_DIM] = o


def _nsa_prompt(q_heads, gate, kvc_blocks_heads, kv_heads, b, t, tq):
    m = b * t
    assert t % tq == 0
    tk = tq
    nt = t // tq
    n_half = kvc_blocks_heads.shape[2]
    n_slc = -(-t // SEL_BLOCK)
    ns_pad = _round_up(n_slc, LANES)
    nkv = N_KV_HEADS

    def kv_spec(branch, which):
        base = branch * 2 * nkv + which * nkv
        return pl.BlockSpec((1, t, HEAD_DIM), lambda bi, ki, i: (base + ki, bi, 0))

    body = functools.partial(_nsa_prompt_body, tq=tq, tk=tk, n_half=n_half, n_slc=n_slc, ns_pad=ns_pad)
    return pl.pallas_call(
        body,
        grid=(b, nkv, nt),
        in_specs=[
            pl.BlockSpec((GQA_GROUP, tq, HEAD_DIM), lambda bi, ki, i: (ki, bi * nt + i, 0)),
            pl.BlockSpec((1, tq, LANES), lambda bi, ki, i: (ki, bi * nt + i, 0)),
            pl.BlockSpec((1, 1, n_half, HEAD_DIM), lambda bi, ki, i: (bi, ki, 0, 0)),
            pl.BlockSpec((1, 1, n_half, HEAD_DIM), lambda bi, ki, i: (bi, nkv + ki, 0, 0)),
            kv_spec(1, 0), kv_spec(1, 1), kv_spec(2, 0), kv_spec(2, 1),
        ],
        out_specs=pl.BlockSpec((tq, GQA_GROUP * HEAD_DIM), lambda bi, ki, i: (bi * nt + i, ki)),
        out_shape=jax.ShapeDtypeStruct((m, NSA_WIDTH), F32),
        scratch_shapes=[pltpu.VMEM((GQA_GROUP, tq, 1), F32), pltpu.VMEM((GQA_GROUP, tq, 1), F32),
                        pltpu.VMEM((GQA_GROUP, tq, HEAD_DIM), F32)],
        compiler_params=_cparams(("parallel", "parallel", "arbitrary"), V7X_VMEM_LIMIT),
        name="nsa_prompt",
    )(q_heads, gate, kvc_blocks_heads, kvc_blocks_heads, kv_heads, kv_heads, kv_heads, kv_heads)


def _nsa_sample_a_body(q_ref, blk_ref, win_ref, new_ref, ocmp_ref, owin_ref, idx_ref, ok_ref,
                       *, qpos, n_half, n_slc, ns_pad, w_buf):
    q = q_ref[0].astype(BF16)
    blk = blk_ref[0]
    kc = blk[:, :_HALF_W].astype(BF16)
    vc = blk[:, _HALF_W:].astype(BF16)
    j = lax.broadcasted_iota(jnp.int32, (1, n_half), 1)
    cmask = (j * CMP_STRIDE + (CMP_BLOCK - 1)) <= qpos
    p = _masked_softmax(_dot_nt(q, kc), cmask)
    ocmp_ref[0] = _dot(p.astype(BF16), vc)

    nh = N_HEADS_NSA
    same = (_fdiv(lax.broadcasted_iota(jnp.int32, (nh, nh), 0), GQA_GROUP)
            == _fdiv(lax.broadcasted_iota(jnp.int32, (nh, nh), 1), GQA_GROUP)).astype(BF16)
    ph, plo = _split_bf16(p)
    psum = _dot(same, ph) + _dot(same, plo)
    ov = _overlap_matrix(n_half, ns_pad, n_slc)
    sh, slo = _split_bf16(psum)
    imp = _dot(sh, ov) + _dot(slo, ov)
    tpos = jnp.full((nh, 1), qpos, jnp.int32)
    score, _ = _block_scores(imp, tpos, n_slc)
    lane = lax.broadcasted_iota(jnp.int32, (1, LANES), 1)
    idx = jnp.zeros((nh, LANES), jnp.int32)
    ok = jnp.zeros((nh, LANES), jnp.int32)
    for r, am, mx, _ in _topk_rounds(score, min(SEL_TOPK, n_slc)):
        idx = jnp.where(lane == r, am.astype(jnp.int32), idx)
        ok = jnp.where(lane == r, (mx > 0.5 * NEG_INF).astype(jnp.int32), ok)
    idx_ref[0] = idx
    ok_ref[0] = ok

    kw_t = win_ref[0, 0].reshape(_HALF_W, w_buf).astype(BF16)
    vw_t = win_ref[0, 1].reshape(_HALF_W, w_buf).astype(BF16)
    r = lax.broadcasted_iota(jnp.int32, (1, w_buf), 1)
    kpos = qpos - w_buf + r
    dist = qpos - kpos
    wm = (dist >= 0) & (dist < WINDOW) & (kpos >= 0)
    sw = jnp.where(wm, _dot(q, kw_t), NEG_INF)
    new = new_ref[0]
    kn = new[:, :_HALF_W]
    vn = new[:, _HALF_W:]
    sn = jnp.sum(q.astype(F32) * kn, axis=-1, keepdims=True)
    mx = jnp.maximum(jnp.max(sw, axis=-1, keepdims=True), sn)
    ew = jnp.where(wm, jnp.exp(sw - mx), 0.0)
    en = jnp.exp(sn - mx)
    den = jnp.sum(ew, axis=-1, keepdims=True) + en
    owin_ref[0] = (_dot_nt(ew.astype(BF16), vw_t) + en * vn) / den


def _nsa_sample_a(qexp, blocks, win_state, new_kvw, qpos):
    bs, n_half, _ = blocks.shape
    w_buf = win_state.shape[-1]
    n_slc = -(-(qpos + 1) // SEL_BLOCK)
    ns_pad = _round_up(n_slc, LANES)
    nh = N_HEADS_NSA
    row3 = lambda bi: (bi, 0, 0)
    body = functools.partial(_nsa_sample_a_body, qpos=qpos, n_half=n_half, n_slc=n_slc,
                             ns_pad=ns_pad, w_buf=w_buf)
    o = jax.ShapeDtypeStruct((bs, nh, LANES), F32)
    oi = jax.ShapeDtypeStruct((bs, nh, LANES), jnp.int32)
    blk = pl.BlockSpec((1, nh, LANES), row3)
    return pl.pallas_call(
        body, grid=(bs,),
        in_specs=[blk, pl.BlockSpec((1, n_half, KV_WIDTH), row3),
                  pl.BlockSpec((1, 2, N_KV_HEADS, HEAD_DIM, w_buf), lambda bi: (bi, 0, 0, 0, 0)),
                  pl.BlockSpec((1, 1, KV_WIDTH), row3)],
        out_specs=(blk, blk, blk, blk), out_shape=(o, o, oi, oi),
        compiler_params=_cparams(("parallel",)),
        name="nsa_sample_cmp_win",
    )(qexp, blocks, win_state, new_kvw)


def _nsa_sample_b_body(idx_ref, ok_ref, pt_ref, q_ref, *refs, n_sel, qpos, n_past_blk, page):
    del pt_ref
    blk_refs = refs[:n_sel]
    new_ref, ocmp_ref, owin_ref, gate_ref, o_ref = refs[n_sel:]
    bi = pl.program_id(0)
    ki = pl.program_id(1)
    per_page = page // SEL_BLOCK
    q = q_ref[0].astype(BF16)
    x = lax.broadcasted_iota(jnp.int32, (1, page), 1)
    scores, vals = [], []
    n_new = jnp.int32(0)
    for jx in range(n_sel):
        bidx = idx_ref[bi, ki, jx]
        good = ok_ref[bi, ki, jx] > 0
        k_t = blk_refs[jx][0, 0, 0].astype(BF16)
        use = good & (bidx < n_past_blk)
        first = jnp.where(use, (bidx % per_page) * SEL_BLOCK, page)
        tok = (bidx // per_page) * page + x
        msk = (x >= first) & (x < first + SEL_BLOCK) & (tok <= qpos)
        scores.append((jnp.where(msk, _dot(q, k_t), NEG_INF), msk))
        vals.append(blk_refs[jx][0, 1, 0].astype(BF16))
        n_new = n_new + jnp.where(good & (bidx == n_past_blk), 1, 0)
    zero = jnp.zeros((1, 1), jnp.int32)
    has_new = (n_new + zero) > 0
    head0 = (ki + zero) == 0
    new = new_ref[0]
    kn = jnp.where(head0, new[:, :HEAD_DIM], new[:, HEAD_DIM:2 * HEAD_DIM])
    vn = jnp.where(head0, new[:, _HALF_W:_HALF_W + HEAD_DIM], new[:, _HALF_W + HEAD_DIM:])
    sn = jnp.where(has_new, jnp.sum(q.astype(F32) * kn, axis=-1, keepdims=True), NEG_INF)
    mx = sn
    for sc, _ in scores:
        mx = jnp.maximum(mx, jnp.max(sc, axis=-1, keepdims=True))
    en = jnp.where(has_new, jnp.exp(sn - mx), 0.0)
    den = en
    acc = en * vn
    for (sc, msk), v_t in zip(scores, vals):
        e = jnp.where(msk, jnp.exp(sc - mx), 0.0)
        den = den + jnp.sum(e, axis=-1, keepdims=True)
        acc = acc + _dot_nt(e.astype(BF16), v_t)
    o_slc = acc * jnp.where(den > 0.0, 1.0 / den, 0.0)
    low = lax.broadcasted_iota(jnp.int32, (N_HEADS_NSA, 1), 0) < GQA_GROUP
    own = lambda y: jnp.where(low, y[:, :HEAD_DIM], y[:, HEAD_DIM:])
    gt = gate_ref[0]
    o = (gt[0, :, :HEAD_DIM] * own(ocmp_ref[0]) + gt[1, :, :HEAD_DIM] * o_slc
         + gt[2, :, :HEAD_DIM] * own(owin_ref[0]))
    o_ref[0, 0] = jnp.where(head0, o[:GQA_GROUP], o[GQA_GROUP:])


def _nsa_sample_b(top_idx, top_ok, page_table, q_rows, cache_t, new_kvs, o_cmp, o_win, gate_exp, qpos):
    bs = q_rows.shape[0]
    page = cache_t.shape[-1]
    per_page = page // SEL_BLOCK
    assert page % SEL_BLOCK == 0
    n_past_blk = page_table.shape[1] * per_page
    n_sel = top_idx.shape[-1]
    nh = N_HEADS_NSA

    def sel_spec(jx):
        def imap(bi, ki, idx, ok, pt):
            blk = jnp.minimum(idx[bi, ki, jx], n_past_blk - 1)
            return (pt[bi, blk // per_page], 0, ki, 0, 0)
        return pl.BlockSpec((1, 2, 1, HEAD_DIM, page), imap)

    row3 = lambda bi, ki, idx, ok, pt: (bi, 0, 0)
    blk = pl.BlockSpec((1, nh, LANES), row3)
    grid_spec = pltpu.PrefetchScalarGridSpec(
        num_scalar_prefetch=3, grid=(bs, N_KV_HEADS),
        in_specs=[pl.BlockSpec((1, nh, HEAD_DIM), row3)] + [sel_spec(jx) for jx in range(n_sel)] + [
            pl.BlockSpec((1, 1, KV_WIDTH), row3), blk, blk,
            pl.BlockSpec((1, N_BRANCH, nh, LANES), lambda bi, ki, idx, ok, pt: (bi, 0, 0, 0))],
        out_specs=pl.BlockSpec((1, 1, GQA_GROUP, HEAD_DIM), lambda bi, ki, idx, ok, pt: (bi, ki, 0, 0)),
    )
    body = functools.partial(_nsa_sample_b_body, n_sel=n_sel, qpos=qpos, n_past_blk=n_past_blk, page=page)
    return pl.pallas_call(
        body, grid_spec=grid_spec,
        out_shape=jax.ShapeDtypeStruct((bs, N_KV_HEADS, GQA_GROUP, HEAD_DIM), F32),
        compiler_params=_cparams(("parallel", "parallel")),
        name="nsa_sample_sel",
    )(top_idx, top_ok, page_table, q_rows, *([cache_t] * n_sel), new_kvs, o_cmp, o_win, gate_exp)


def _hgrn_prompt_body(q_ref, f_ref, v_ref, g_ref, nw_ref, o_ref, s_ref, st_ref, *, tc):
    c = HG_CHUNK
    i = pl.program_id(2)

    @pl.when(i == 0)
    def _():
        st_ref[...] = jnp.zeros(st_ref.shape, F32)

    f = f_ref[...]
    lf = jnp.log(f)
    k = 1.0 - f
    v = v_ref[...]
    r = lax.broadcasted_iota(jnp.int32, (tc, HG_DK), 0) & (c - 1)
    cum = lf
    sh = 1
    while sh < c:
        cum = cum + jnp.where(r >= sh, pltpu.roll(cum, sh, axis=0), 0.0)
        sh *= 2
    q_dec = (q_ref[...] * jnp.exp(cum)).astype(BF16)
    k_inv = (k * jnp.exp(-cum)).astype(BF16)
    causal = (lax.broadcasted_iota(jnp.int32, (c, c), 0) >= lax.broadcasted_iota(jnp.int32, (c, c), 1))
    nw = nw_ref[...]
    st = st_ref[...]
    for ci in range(tc // c):
        sl = slice(ci * c, (ci + 1) * c)
        cum_c = cum[sl]
        last = cum_c[c - 1:c, :]
        vc = v[sl].astype(BF16)
        k_dec = (k[sl] * jnp.exp(last - cum_c)).astype(BF16)
        attn = jnp.where(causal, _dot_nt(q_dec[sl], k_inv[sl]), 0.0)
        o = _dot_nt(q_dec[sl], st.astype(BF16)) + _dot(attn.astype(BF16), vc)
        st = st * jnp.exp(last) + _dot_tn(vc, k_dec)
        o_ref[sl, :] = (_rms(o, nw) * g_ref[sl, :]).astype(o_ref.dtype)
    st_ref[...] = st

    @pl.when(i == pl.num_programs(2) - 1)
    def _():
        s_ref[0, 0] = st.T


def _hgrn_prompt(hq, hf, hv, hg, norm_w, b, t, tc):
    m = b * t
    assert t % tc == 0 and tc % HG_CHUNK == 0
    nt = t // tc
    blk = pl.BlockSpec((tc, HG_DK), lambda bi, hi, i: (bi * nt + i, hi))
    return pl.pallas_call(
        functools.partial(_hgrn_prompt_body, tc=tc),
        grid=(b, HG_HEADS, nt),
        in_specs=[blk, blk, blk, blk, pl.BlockSpec((1, HG_DV), lambda bi, hi, i: (0, hi))],
        out_specs=(blk, pl.BlockSpec((1, 1, HG_DK, HG_DV), lambda bi, hi, i: (bi, hi, 0, 0))),
        out_shape=(jax.ShapeDtypeStruct((m, HG_WIDTH), F32),
                   jax.ShapeDtypeStruct((b, HG_HEADS, HG_DK, HG_DV), F32)),
        scratch_shapes=[pltpu.VMEM((HG_DV, HG_DK), F32)],
        compiler_params=_cparams(("parallel", "parallel", "arbitrary")),
        name="hgrn_prompt",
    )(hq, hf, hv, hg, norm_w.reshape(1, HG_WIDTH))


def _hgrn_sample_body(q_ref, f_ref, v_ref, g_ref, nw_ref, s0_ref, o_ref, s_ref):
    def column(x):
        return jnp.broadcast_to(x, (HG_DV, HG_DK)).T

    for h in range(HG_HEADS):
        sl = slice(h * HG_DK, (h + 1) * HG_DK)
        q = q_ref[0, :, sl]
        f = f_ref[0, :, sl]
        v = v_ref[0, :, sl]
        k = 1.0 - f
        s0 = s0_ref[0, h]
        qk = jnp.sum(q * k, axis=-1, keepdims=True)
        o = jnp.sum(column(q * f) * s0, axis=0, keepdims=True) + qk * v
        s_ref[0, h] = column(f) * s0 + column(k) * v
        o_ref[0, :, sl] = _rms(o, nw_ref[:, sl]) * g_ref[0, :, sl]


def _hgrn_sample(hq, hf, hv, hg, norm_w, s0):
    bs = hq.shape[0]
    r3 = lambda x: x.reshape(bs, 1, HG_WIDTH)
    row = pl.BlockSpec((1, 1, HG_WIDTH), lambda bi: (bi, 0, 0))
    st = pl.BlockSpec((1, HG_HEADS, HG_DK, HG_DV), lambda bi: (bi, 0, 0, 0))
    o, s = pl.pallas_call(
        _hgrn_sample_body, grid=(bs,),
        in_specs=[row, row, row, row, pl.BlockSpec((1, HG_WIDTH), lambda bi: (0, 0)), st],
        out_specs=(row, st),
        out_shape=(jax.ShapeDtypeStruct((bs, 1, HG_WIDTH), F32),
                   jax.ShapeDtypeStruct(s0.shape, F32)),
        compiler_params=_cparams(("parallel",)),
        name="hgrn_sample",
    )(r3(hq), r3(hf), r3(hv), r3(hg), norm_w.reshape(1, HG_WIDTH), s0)
    return o.reshape(bs, HG_WIDTH), s


def _tail_body(h_ref, on_ref, oh_ref, p_ref, gn_ref, wo_ref, gf_ref, wgu_ref, wd_ref, gp_ref,
               wpg_ref, wpp_ref, gl_ref, y_ref, *, d_ff, ff_chunk):
    a = _rms(on_ref[...], gn_ref[...]).astype(BF16)
    b = oh_ref[...].astype(BF16)
    nsa_w = on_ref.shape[-1]
    h = h_ref[...] + _dot(a, wo_ref[:nsa_w, :]) + _dot(b, wo_ref[nsa_w:, :])
    x = _rms(h, gf_ref[...]).astype(BF16)
    for c0 in range(0, d_ff, ff_chunk):
        gate = _dot(x, wgu_ref[:, c0:c0 + ff_chunk])
        up = _dot(x, wgu_ref[:, d_ff + c0:d_ff + c0 + ff_chunk])
        h = h + _dot((_silu(gate) * up).astype(BF16), wd_ref[c0:c0 + ff_chunk, :])
    x = _rms(h, gp_ref[...]).astype(BF16)
    h = h + _sigmoid(_dot(x, wpg_ref[...])) * _dot(p_ref[...].astype(BF16), wpp_ref[...])
    y_ref[...] = _rms(h, gl_ref[...])


def _tail(h, o_nsa, o_hg, p_emb, norm_nsa, w_out, norm_ffn, w_gate_up, w_down, norm_ple, w_ple_gate,
          w_ple_proj, norm_final, tm):
    m, d = h.shape
    assert m % tm == 0
    d_ff = w_down.shape[0]
    ff_chunk = d_ff
    row = lambda i: (i, 0)
    const = lambda i: (0, 0)
    vec = lambda g: g.reshape(1, -1)
    full = lambda w: pl.BlockSpec(w.shape, const)
    blk = lambda x: pl.BlockSpec((tm, x.shape[1]), row)
    args = (h, o_nsa, o_hg, p_emb, vec(norm_nsa), w_out, vec(norm_ffn), w_gate_up, w_down,
            vec(norm_ple), w_ple_gate, w_ple_proj, vec(norm_final))
    in_specs = [blk(h), blk(o_nsa), blk(o_hg), blk(p_emb)] + [full(w) for w in args[4:]]
    return pl.pallas_call(
        functools.partial(_tail_body, d_ff=d_ff, ff_chunk=ff_chunk),
        grid=(m // tm,), in_specs=in_specs, out_specs=pl.BlockSpec((tm, d), row),
        out_shape=jax.ShapeDtypeStruct((m, d), F32),
        compiler_params=_cparams(("parallel",), V7X_VMEM_LIMIT),
        name="tail",
    )(*args)


def _tile(m, pref):
    return pref if m % pref == 0 else m


def kernel(x_prompt, x_sample, cache_kv_cmp, cache_kv_slc, state_kv_win, state_hgrn, page_table,
           p_prompt, p_sample, norm_mix, w_in, cmp_pos, cmp_w1, cmp_w2, hg_lb, norm_nsa_out,
           norm_hg_out, w_out, norm_ffn, w_gate_up, w_down, norm_ple, w_ple_gate, w_ple_proj,
           norm_final):
    depth = w_in.shape[0]
    assert depth == 1
    l = 0
    b, t, d = x_prompt.shape
    bs, ts, _ = x_sample.shape
    assert ts == 1
    n_pool, page = cache_kv_cmp.shape[1:3]
    n_pages = page_table.shape[1]
    past = n_pages * page
    w_buf = state_kv_win.shape[2]
    nkv = N_KV_HEADS

    lb = jnp.cumsum(jax.nn.softmax(hg_lb.astype(F32), axis=0), axis=0)[l]
    w_packed = _pack_w_in(w_in[l])
    wbd, prow, w2bd = _pack_cmp_weights(cmp_pos[l], cmp_w1[l], cmp_w2[l])
    tail_w = (norm_nsa_out[l], w_out[l].astype(BF16), norm_ffn[l], w_gate_up[l].astype(BF16),
              w_down[l].astype(BF16), norm_ple[l], w_ple_gate[l].astype(BF16),
              w_ple_proj[l].astype(BF16), norm_final)

    m = b * t
    xp = x_prompt.reshape(m, d)
    (q_h, kvt, kv_h, gate, hq, hf, hv, hg) = _inproj(xp, norm_mix[l], w_packed, lb, _tile(m, 256), seq=(b, t))
    kvt = kvt.reshape(N_BRANCH, b, 2, nkv, HEAD_DIM, t)
    pages_p = t // page
    pt_p = jnp.zeros((b, pages_p), jnp.int32)
    blocks_p = _compress(kvt[0], pt_p, lambda pt, bi, p: (bi, 0, 0, 0, p), page, wbd, prow, w2bd)
    n_half_p = blocks_p.shape[1]
    blocks_ph = blocks_p.reshape(b, n_half_p, 2 * nkv, HEAD_DIM).transpose(0, 2, 1, 3).astype(BF16)
    o_nsa_p = _nsa_prompt(q_h, gate, blocks_ph, kv_h, b, t, _tile(t, 128))
    o_hg_p, s_p = _hgrn_prompt(hq, hf, hv, hg, norm_hg_out[l], b, t, _tile(t, 256))
    y_p = _tail(xp, o_nsa_p, o_hg_p, p_prompt[l].reshape(m, -1), *tail_w, tm=_tile(m, 256))

    xs = x_sample.reshape(bs, d)
    (q_hs, kv_s, _, gate_s, hq_s, hf_s, hv_s, hg_s) = _inproj(xs, norm_mix[l], w_packed, lb, bs)
    token_minor = lambda a: a.transpose(0, 2, 3, 4, 1)
    blocks_s = _compress(token_minor(cache_kv_cmp[l]), page_table, lambda pt, bi, p: (pt[bi, p], 0, 0, 0, 0),
                         page, wbd, prow, w2bd)
    q_rows = q_hs.astype(F32).transpose(1, 0, 2)
    lane_half = (jnp.arange(N_HEADS_NSA) // GQA_GROUP)[:, None] == jnp.arange(nkv)[None, :]
    qexp = (q_rows[:, :, None, :] * lane_half[None, :, :, None]).reshape(bs, N_HEADS_NSA, LANES)
    o_cmp_s, o_win_s, idx_s, ok_s = _nsa_sample_a(
        qexp, blocks_s, token_minor(state_kv_win[l]), kv_s[2].reshape(bs, 1, KV_WIDTH), past)
    n_sel = min(SEL_TOPK, -(-(past + 1) // SEL_BLOCK))
    top_idx = idx_s[:, ::GQA_GROUP, :n_sel]
    top_ok = ok_s[:, ::GQA_GROUP, :n_sel]
    gsm = gate_s[:, :, :N_BRANCH * GQA_GROUP].reshape(nkv, bs, N_BRANCH, GQA_GROUP)
    gate_exp = jnp.broadcast_to(gsm.transpose(1, 2, 0, 3).reshape(bs, N_BRANCH, N_HEADS_NSA, 1),
                                (bs, N_BRANCH, N_HEADS_NSA, LANES))
    o_sel = _nsa_sample_b(top_idx, top_ok, page_table, q_rows, token_minor(cache_kv_slc[l]),
                          kv_s[1].reshape(bs, 1, KV_WIDTH), o_cmp_s, o_win_s, gate_exp, past)
    o_nsa_s = o_sel.reshape(bs, NSA_WIDTH)
    o_hg_s, s_s = _hgrn_sample(hq_s, hf_s, hv_s, hg_s, norm_hg_out[l], state_hgrn[l])
    y_s = _tail(xs, o_nsa_s, o_hg_s, p_sample[l].reshape(bs, -1), *tail_w, tm=bs)

    rows_major = lambda a: a.transpose(0, 4, 1, 2, 3)[None]
    kv6 = lambda a: a.reshape(1, bs, 1, 2, nkv, HEAD_DIM)
    win_s = jnp.concatenate([state_kv_win[l], kv_s[2].reshape(bs, 1, 2, nkv, HEAD_DIM)], axis=1)[:, -w_buf:]
    return (y_p.reshape(b, t, d), y_s.reshape(bs, ts, d),
            rows_major(kvt[0]), rows_major(kvt[1]), rows_major(kvt[2][..., t - min(WINDOW, t):]), s_p[None],
            kv6(kv_s[0]), kv6(kv_s[1]), win_s[None], s_s[None])
```

```python
import functools

import jax
import jax.numpy as jnp
from jax import lax
from jax.experimental import pallas as pl
from jax.experimental.pallas import tpu as pltpu

F32 = jnp.float32
BF16 = jnp.bfloat16

N_HEADS_NSA = 8
HEAD_DIM = 64
N_KV_HEADS = 2
GQA_GROUP = N_HEADS_NSA // N_KV_HEADS
CMP_BLOCK = 32
CMP_STRIDE = 16
SEL_BLOCK = 64
SEL_TOPK = 16
WINDOW = 512
N_BRANCH = 3
HG_HEADS = 4
HG_DK = 128
HG_DV = 128
HG_CHUNK = 32
SCALE = HEAD_DIM ** -0.5
NEG_INF = -1e30
FORCED_SCORE = 1e9
EPS = 1e-6
BELOW_ALL = -3.0e38

NSA_WIDTH = N_HEADS_NSA * HEAD_DIM
KV_WIDTH = 2 * N_KV_HEADS * HEAD_DIM
HG_WIDTH = HG_HEADS * HG_DK
LANES = 128
_HALF_W = N_KV_HEADS * HEAD_DIM
V7X_VMEM_LIMIT = 56 * 1024 * 1024


def _cparams(sem, vmem=None):
    return pltpu.CompilerParams(dimension_semantics=sem, vmem_limit_bytes=vmem)


def _sigmoid(x):
    return 1.0 / (1.0 + jnp.exp(-x))


def _silu(x):
    return x * _sigmoid(x)


def _rms(x, g):
    return x * lax.rsqrt(jnp.mean(x * x, axis=-1, keepdims=True) + EPS) * g


def _dot(a, b):
    return jnp.dot(a, b, preferred_element_type=F32)


def _dot_nt(a, b):
    return lax.dot_general(a, b, (((1,), (1,)), ((), ())), preferred_element_type=F32)


def _dot_tn(a, b):
    return lax.dot_general(a, b, (((0,), (0,)), ((), ())), preferred_element_type=F32)


def _split_bf16(x):
    hi = x.astype(BF16)
    lo = (x - hi.astype(F32)).astype(BF16)
    return hi, lo


def _round_up(x, m):
    return -(-x // m) * m


def _fdiv(x, n):
    assert n & (n - 1) == 0
    return jnp.right_shift(x, n.bit_length() - 1)


_OFF_KV = NSA_WIDTH
_OFF_HG = NSA_WIDTH + N_BRANCH * KV_WIDTH
_OFF_GATE = _OFF_HG + 4 * HG_WIDTH
_W_PACKED = _OFF_GATE + N_KV_HEADS * LANES


def _pack_w_in(w_in):
    d = w_in.shape[0]
    a = w_in[:, :_OFF_HG]
    gate = w_in[:, _OFF_HG:_OFF_HG + N_BRANCH * N_HEADS_NSA]
    b = w_in[:, _OFF_HG + N_BRANCH * N_HEADS_NSA:]
    gate = gate.reshape(d, N_BRANCH, N_KV_HEADS, GQA_GROUP).transpose(0, 2, 1, 3)
    gate = gate.reshape(d, N_KV_HEADS, N_BRANCH * GQA_GROUP)
    gate = jnp.pad(gate, ((0, 0), (0, 0), (0, LANES - N_BRANCH * GQA_GROUP)))
    return jnp.concatenate([a, b, gate.reshape(d, N_KV_HEADS * LANES)], axis=1).astype(BF16)


def _inproj_body(x_ref, g_ref, w_ref, lb_ref, *out_refs, token_minor):
    xn = _rms(x_ref[...], g_ref[...]).astype(BF16)
    tm = xn.shape[0]

    def seg(lo, n):
        return _dot(xn, w_ref[:, lo:lo + n])

    qa = seg(0, NSA_WIDTH) * SCALE
    kvs = [seg(_OFF_KV + br * KV_WIDTH, KV_WIDTH) for br in range(N_BRANCH)]
    gates = [_sigmoid(seg(_OFF_GATE + k * LANES, LANES)) for k in range(N_KV_HEADS)]
    if token_minor:
        q_ref, kv_ref, krow_ref, vtile_ref, gate_ref, hq_ref, hf_ref, hv_ref, hg_ref = out_refs
        q_ref[...] = qa.T.reshape(N_HEADS_NSA, HEAD_DIM, tm).astype(BF16)
        for br, kv in enumerate(kvs):
            kv_t = kv.T
            kv_ref[br, 0] = kv_t.reshape(2 * N_KV_HEADS, HEAD_DIM, tm)
            if br > 0:
                krow_ref[br - 1] = kv[:, :_HALF_W].astype(BF16)
                for jt in range(tm // LANES):
                    vtile_ref[br - 1, jt] = kv_t[_HALF_W:, jt * LANES:(jt + 1) * LANES].astype(BF16)
        for k, gate in enumerate(gates):
            gate_ref[k] = gate.T[:16]
    else:
        q_ref, kv_ref, gate_ref, hq_ref, hf_ref, hv_ref, hg_ref = out_refs
        for h in range(N_HEADS_NSA):
            q_ref[h] = qa[:, h * HEAD_DIM:(h + 1) * HEAD_DIM].astype(BF16)
        for br, kv in enumerate(kvs):
            kv_ref[br] = kv
        for k, gate in enumerate(gates):
            gate_ref[k] = gate
    hq_ref[...] = _silu(seg(_OFF_HG, HG_WIDTH))
    lb = lb_ref[...]
    hf_ref[...] = lb + (1.0 - lb) * _sigmoid(seg(_OFF_HG + HG_WIDTH, HG_WIDTH))
    hv_ref[...] = seg(_OFF_HG + 2 * HG_WIDTH, HG_WIDTH)
    hg_ref[...] = _silu(seg(_OFF_HG + 3 * HG_WIDTH, HG_WIDTH))


def _inproj(x, g, w_packed, lb, tm, seq=None):
    m, d = x.shape
    assert m % tm == 0
    row = lambda i: (i, 0)
    const = lambda i: (0, 0)
    mid = lambda i: (0, i, 0)
    last = lambda i: (0, 0, i)
    sds = jax.ShapeDtypeStruct
    nkv2 = 2 * N_KV_HEADS
    if seq is None:
        shapes = [sds((N_HEADS_NSA, m, HEAD_DIM), BF16), sds((N_BRANCH, m, KV_WIDTH), F32),
                  sds((N_KV_HEADS, m, LANES), F32)]
        specs = [pl.BlockSpec((N_HEADS_NSA, tm, HEAD_DIM), mid), pl.BlockSpec((N_BRANCH, tm, KV_WIDTH), mid),
                 pl.BlockSpec((N_KV_HEADS, tm, LANES), mid)]
    else:
        b, t = seq
        assert b * t == m and t % tm == 0 and tm % LANES == 0
        nt = t // tm
        shapes = [sds((N_HEADS_NSA, HEAD_DIM, m), BF16), sds((N_BRANCH, b, nkv2, HEAD_DIM, t), F32),
                  sds((2, m, _HALF_W), BF16), sds((2, m // LANES, _HALF_W, LANES), BF16),
                  sds((N_KV_HEADS, 16, m), F32)]
        specs = [pl.BlockSpec((N_HEADS_NSA, HEAD_DIM, tm), last),
                 pl.BlockSpec((N_BRANCH, 1, nkv2, HEAD_DIM, tm), lambda i: (0, i // nt, 0, 0, i % nt)),
                 pl.BlockSpec((2, tm, _HALF_W), mid),
                 pl.BlockSpec((2, tm // LANES, _HALF_W, LANES), lambda i: (0, i, 0, 0)),
                 pl.BlockSpec((N_KV_HEADS, 16, tm), last)]
    shapes += [sds((m, HG_WIDTH), F32)] * 4
    specs += [pl.BlockSpec((tm, HG_WIDTH), row)] * 4
    return pl.pallas_call(
        functools.partial(_inproj_body, token_minor=seq is not None),
        grid=(m // tm,),
        in_specs=[pl.BlockSpec((tm, d), row), pl.BlockSpec((1, d), const),
                  pl.BlockSpec((d, _W_PACKED), const), pl.BlockSpec((1, HG_WIDTH), const)],
        out_specs=tuple(specs), out_shape=tuple(shapes),
        compiler_params=_cparams(("parallel",), V7X_VMEM_LIMIT),
        name="inproj",
    )(x, g.reshape(1, d), w_packed, lb.reshape(1, HG_WIDTH))


_CMP_K = CMP_STRIDE * _HALF_W
_PAGES_PER_STEP = 8


def _pack_cmp_weights(cmp_pos, cmp_w1, cmp_w2):
    hid = cmp_w1.shape[-1]
    w1h = cmp_w1.reshape(2, 2, CMP_STRIDE, HEAD_DIM, hid)
    eye = jnp.eye(N_KV_HEADS, dtype=cmp_w1.dtype)
    wbd = jnp.einsum('ctsdf,hg->cshdtgf', w1h, eye).reshape(2, _CMP_K, 2 * N_KV_HEADS * hid)
    posh = cmp_pos.reshape(2, 2, CMP_STRIDE, 1, HEAD_DIM)
    prow = jnp.broadcast_to(posh, (2, 2, CMP_STRIDE, N_KV_HEADS, HEAD_DIM)).reshape(2, 2, _CMP_K)
    prow = jnp.pad(prow, ((0, 0), (0, 6), (0, 0)))
    w2bd = jnp.einsum('cfd,hg->chfgd', cmp_w2, eye).reshape(2, N_KV_HEADS * hid, _HALF_W)
    return wbd.astype(BF16), prow.astype(BF16), w2bd.astype(BF16)


def _compress_body(pt_ref, *refs, n_pg, n_half, halves_per_page, hid2):
    del pt_ref
    page_refs = refs[:n_pg]
    wbd_ref, prow_ref, w2_ref, out_ref, out_t_ref, a_ref, t_ref = refs[n_pg:]
    g = pl.program_id(1)
    page = halves_per_page * CMP_STRIDE
    for k in range(n_pg):
        base = pl.multiple_of((g * n_pg + k) * halves_per_page, halves_per_page)
        for c in range(2):
            t_ref[k, c] = page_refs[k][0, c].reshape(_HALF_W, page).T
            for s in range(CMP_STRIDE):
                a_ref[c, pl.ds(base, halves_per_page), s * _HALF_W:(s + 1) * _HALF_W] = (
                    t_ref[k, c, pl.ds(s, halves_per_page, stride=CMP_STRIDE), :])

    @pl.when(g == pl.num_programs(1) - 1)
    def _():
        col = lax.broadcasted_iota(jnp.int32, (1, 2 * hid2), 1)
        for c in range(2):
            z = _dot(a_ref[c].astype(BF16), wbd_ref[c])
            zp = _dot(prow_ref[c], wbd_ref[c])
            z = z + jnp.where(col < hid2, zp[0:1], zp[1:2])
            hi_next = pltpu.roll(z[:, hid2:], n_half - 1, axis=0)
            hidden = _silu(z[:, :hid2] + hi_next)
            blocks = _dot(hidden.astype(BF16), w2_ref[c])
            out_ref[0, :, c * _HALF_W:(c + 1) * _HALF_W] = blocks
            out_t_ref[0, c * _HALF_W:(c + 1) * _HALF_W, :] = blocks.T


def _compress(pages_t, page_table, page_index, page, wbd, prow, w2bd):
    b, n_pages = page_table.shape
    width = KV_WIDTH
    assert page % CMP_STRIDE == 0 and page % LANES == 0
    n_pg = min(_PAGES_PER_STEP, n_pages)
    assert n_pages % n_pg == 0
    halves_per_page = page // CMP_STRIDE
    n_half = n_pages * halves_per_page
    hid2 = wbd.shape[-1] // 2

    def page_spec(k):
        return pl.BlockSpec((1, 2, N_KV_HEADS, HEAD_DIM, page),
                            lambda bi, gi, pt: page_index(pt, bi, gi * n_pg + k))

    const3 = lambda bi, gi, pt: (0, 0, 0)
    grid_spec = pltpu.PrefetchScalarGridSpec(
        num_scalar_prefetch=1,
        grid=(b, n_pages // n_pg),
        in_specs=[page_spec(k) for k in range(n_pg)] + [
            pl.BlockSpec(wbd.shape, const3), pl.BlockSpec(prow.shape, const3),
            pl.BlockSpec(w2bd.shape, const3)],
        out_specs=(pl.BlockSpec((1, n_half, width), lambda bi, gi, pt: (bi, 0, 0)),
                   pl.BlockSpec((1, width, n_half), lambda bi, gi, pt: (bi, 0, 0))),
        scratch_shapes=[pltpu.VMEM((2, n_half, _CMP_K), F32), pltpu.VMEM((n_pg, 2, page, _HALF_W), F32)],
    )
    body = functools.partial(_compress_body, n_pg=n_pg, n_half=n_half,
                             halves_per_page=halves_per_page, hid2=hid2)
    return pl.pallas_call(
        body, grid_spec=grid_spec,
        out_shape=(jax.ShapeDtypeStruct((b, n_half, width), F32),
                   jax.ShapeDtypeStruct((b, width, n_half), F32)),
        compiler_params=_cparams(("parallel", "arbitrary"), V7X_VMEM_LIMIT),
        name="compress",
    )(page_table, *([pages_t] * n_pg), wbd, prow, w2bd)


def _iota_along(n, axis):
    shape = (1, n) if axis == 1 else (n, 1)
    return lax.broadcasted_iota(jnp.int32, shape, axis)


def _overlap_matrix(n_half, ns_pad, n_slc, slc_axis=1):
    cs = _iota_along(n_half, 1 - slc_axis) * CMP_STRIDE
    m = _iota_along(ns_pad, slc_axis)
    ss = m * SEL_BLOCK
    return ((cs < ss + SEL_BLOCK) & (cs + CMP_BLOCK > ss) & (m < n_slc)).astype(BF16)


def _block_scores(imp, tpos, n_slc, axis=1):
    m = _iota_along(imp.shape[axis], axis)
    qblk = _fdiv(tpos, SEL_BLOCK)
    forced = (m == 0) | (m == qblk) | (m == qblk - 1)
    valid = (m * SEL_BLOCK <= tpos) & (m < n_slc)
    score = jnp.where(valid, jnp.where(forced, FORCED_SCORE, imp), NEG_INF)
    return jnp.where(m < n_slc, score, BELOW_ALL), valid


def _topk_rounds(score, n_rounds, axis=1):
    idx = _iota_along(score.shape[axis], axis).astype(F32)
    sc = score
    for r in range(n_rounds):
        mx = jnp.max(sc, axis=axis, keepdims=True)
        am = jnp.min(jnp.where(sc == mx, idx, 1e9), axis=axis, keepdims=True)
        hit = idx == am
        sc = jnp.where(hit, BELOW_ALL, sc)
        yield r, am, mx, hit


def _masked_softmax(s, mask):
    s = jnp.where(mask, s, NEG_INF)
    mx = jnp.max(s, axis=-1, keepdims=True)
    e = jnp.where(mask, jnp.exp(s - mx), 0.0)
    den = jnp.sum(e, axis=-1, keepdims=True)
    return e * jnp.where(den > 0.0, 1.0 / den, 0.0)


def _nsa_prompt_body(q_ref, gate_ref, kc_ref, vct_ref, ks_ref, vs_ref, kw_ref, vw_ref, o_ref, acc_ref,
                     *, tq, tk, n_half, n_slc, ns_pad):
    g4 = GQA_GROUP
    ki = pl.program_id(1)
    t0 = pl.program_id(2) * tq
    zero = jnp.zeros((1, 1), jnp.int32)
    head0 = (ki + zero) == 0
    qt = jnp.concatenate([q_ref[g] for g in range(g4)], axis=1).astype(F32)
    qx = jnp.concatenate([jnp.where(head0, qt, 0.0), jnp.where(head0, 0.0, qt)], axis=0).astype(BF16)
    tpos = t0 + lax.broadcasted_iota(jnp.int32, (1, tq), 1)
    lanes4 = lambda x: jnp.concatenate([x] * g4, axis=1)

    st = _dot(kc_ref[0].astype(BF16), qx)
    j = lax.broadcasted_iota(jnp.int32, (n_half, 1), 0)
    cmask = lanes4(jnp.where((j * CMP_STRIDE + (CMP_BLOCK - 1)) <= tpos, 1.0, 0.0)) > 0.5
    st = jnp.where(cmask, st, NEG_INF)
    e = jnp.where(cmask, jnp.exp(st - jnp.max(st, axis=0, keepdims=True)), 0.0)
    den = jnp.sum(e, axis=0, keepdims=True)
    pt = e * jnp.where(den > 0.0, 1.0 / den, 0.0)
    oc_both = _dot(vct_ref[0].astype(BF16), pt.astype(BF16))
    o_cmp = jnp.where(head0, oc_both[:HEAD_DIM], oc_both[HEAD_DIM:])

    psum = pt[:, :tq]
    for g in range(1, g4):
        psum = psum + pt[:, g * tq:(g + 1) * tq]
    ov = _overlap_matrix(n_half, ns_pad, n_slc, slc_axis=0)
    ph, plo = _split_bf16(psum)
    imp = _dot(ov, ph) + _dot(ov, plo)
    score, valid = _block_scores(imp, tpos, n_slc, axis=0)
    sel = jnp.zeros(score.shape, F32)
    for _, _, _, hit in _topk_rounds(score, min(SEL_TOPK, n_slc), axis=0):
        sel = jnp.where(hit, 1.0, sel)
    selb = jnp.where(valid, sel, 0.0).astype(BF16)

    row0 = pl.multiple_of(ki * HEAD_DIM, HEAD_DIM)

    def flash(k_ref, v_ref, n_lo, n_hi, allowed):
        acc_ref[...] = jnp.zeros(acc_ref.shape, F32)

        def body(n, carry):
            m_old, l_old = carry
            kpos = n * tk + lax.broadcasted_iota(jnp.int32, (tk, 1), 0)
            bias = lanes4(jnp.where(allowed(kpos), 0.0, NEG_INF))
            sc = _dot(k_ref[0, n], qx) + bias
            m_new = jnp.maximum(m_old, jnp.max(sc, axis=0, keepdims=True))
            e = jnp.exp(sc - m_new)
            alpha = jnp.exp(m_old - m_new)
            l_new = alpha * l_old + jnp.sum(e, axis=0, keepdims=True)
            vt = v_ref[0, n, pl.ds(row0, HEAD_DIM), :]
            acc_ref[...] = alpha * acc_ref[...] + _dot(vt, e.astype(BF16))
            return m_new, l_new

        init = (jnp.full((1, g4 * tq), NEG_INF, F32), jnp.zeros((1, g4 * tq), F32))
        _, l = lax.fori_loop(n_lo, n_hi, body, init)
        return acc_ref[...] * jnp.where(l > 0.0, 1.0 / l, 0.0)

    def sel_allowed(kpos):
        m = lax.broadcasted_iota(jnp.int32, (1, ns_pad), 1)
        expand = (m == _fdiv(kpos, SEL_BLOCK)).astype(BF16)
        return (_dot(expand, selb) > 0.5) & (kpos <= tpos)

    def win_allowed(kpos):
        dist = tpos - kpos
        return (dist >= 0) & (dist < WINDOW)

    n_hi = (t0 + tq + tk - 1) // tk
    o_slc = flash(ks_ref, vs_ref, 0, n_hi, sel_allowed)
    o_win = flash(kw_ref, vw_ref, jnp.maximum(t0 - (WINDOW - 1), 0) // tk, n_hi, win_allowed)

    gt = gate_ref[0]
    heads = []
    for g in range(g4):
        sl = slice(g * tq, (g + 1) * tq)
        heads.append(gt[g:g + 1] * o_cmp[:, sl] + gt[g4 + g:g4 + g + 1] * o_slc[:, sl]
                     + gt[2 * g4 + g:2 * g4 + g + 1] * o_win[:, sl])
    o_ref[...] = jnp.concatenate(heads, axis=0).T


def _nsa_prompt(q_t, gate_t, blocks, blocks_t, k_rows, v_tiles, b, t, tq):
    m = b * t
    tk = LANES
    assert t % tq == 0 and tq % tk == 0
    nt = t // tq
    n_half = blocks.shape[1]
    n_slc = -(-t // SEL_BLOCK)
    ns_pad = _round_up(n_slc, 8)
    nkv = N_KV_HEADS
    k_tiles = k_rows.reshape(2, m // tk, tk, _HALF_W)
    seq_k = lambda br: pl.BlockSpec((1, t // tk, tk, _HALF_W), lambda bi, ki, i: (br, bi, 0, 0))
    seq_v = lambda br: pl.BlockSpec((1, t // tk, _HALF_W, tk), lambda bi, ki, i: (br, bi, 0, 0))
    body = functools.partial(_nsa_prompt_body, tq=tq, tk=tk, n_half=n_half, n_slc=n_slc, ns_pad=ns_pad)
    return pl.pallas_call(
        body,
        grid=(b, nkv, nt),
        in_specs=[
            pl.BlockSpec((GQA_GROUP, HEAD_DIM, tq), lambda bi, ki, i: (ki, 0, bi * nt + i)),
            pl.BlockSpec((1, 16, tq), lambda bi, ki, i: (ki, 0, bi * nt + i)),
            pl.BlockSpec((1, n_half, _HALF_W), lambda bi, ki, i: (bi, 0, 0)),
            pl.BlockSpec((1, _HALF_W, n_half), lambda bi, ki, i: (bi, 1, 0)),
            seq_k(0), seq_v(0), seq_k(1), seq_v(1),
        ],
        out_specs=pl.BlockSpec((tq, GQA_GROUP * HEAD_DIM), lambda bi, ki, i: (bi * nt + i, ki)),
        out_shape=jax.ShapeDtypeStruct((m, NSA_WIDTH), F32),
        scratch_shapes=[pltpu.VMEM((HEAD_DIM, GQA_GROUP * tq), F32)],
        compiler_params=_cparams(("parallel", "parallel", "arbitrary"), V7X_VMEM_LIMIT),
        name="nsa_prompt",
    )(q_t, gate_t, blocks, blocks_t, k_tiles, v_tiles, k_tiles, v_tiles)


def _nsa_sample_a_body(q_ref, blk_ref, win_ref, new_ref, ocmp_ref, owin_ref, idx_ref, ok_ref,
                       *, qpos, n_half, n_slc, ns_pad, w_buf):
    q = q_ref[0].astype(BF16)
    blk = blk_ref[0]
    kc = blk[:, :_HALF_W].astype(BF16)
    vc = blk[:, _HALF_W:].astype(BF16)
    j = lax.broadcasted_iota(jnp.int32, (1, n_half), 1)
    cmask = (j * CMP_STRIDE + (CMP_BLOCK - 1)) <= qpos
    p = _masked_softmax(_dot_nt(q, kc), cmask)
    ocmp_ref[0] = _dot(p.astype(BF16), vc)

    nh = N_HEADS_NSA
    same = (_fdiv(lax.broadcasted_iota(jnp.int32, (nh, nh), 0), GQA_GROUP)
            == _fdiv(lax.broadcasted_iota(jnp.int32, (nh, nh), 1), GQA_GROUP)).astype(BF16)
    ph, plo = _split_bf16(p)
    psum = _dot(same, ph) + _dot(same, plo)
    ov = _overlap_matrix(n_half, ns_pad, n_slc)
    sh, slo = _split_bf16(psum)
    imp = _dot(sh, ov) + _dot(slo, ov)
    tpos = jnp.full((nh, 1), qpos, jnp.int32)
    score, _ = _block_scores(imp, tpos, n_slc)
    lane = lax.broadcasted_iota(jnp.int32, (1, LANES), 1)
    idx = jnp.zeros((nh, LANES), jnp.int32)
    ok = jnp.zeros((nh, LANES), jnp.int32)
    for r, am, mx, _ in _topk_rounds(score, min(SEL_TOPK, n_slc)):
        idx = jnp.where(lane == r, am.astype(jnp.int32), idx)
        ok = jnp.where(lane == r, (mx > 0.5 * NEG_INF).astype(jnp.int32), ok)
    idx_ref[0] = idx
    ok_ref[0] = ok

    kw_t = win_ref[0, 0].reshape(_HALF_W, w_buf).astype(BF16)
    vw_t = win_ref[0, 1].reshape(_HALF_W, w_buf).astype(BF16)
    r = lax.broadcasted_iota(jnp.int32, (1, w_buf), 1)
    kpos = qpos - w_buf + r
    dist = qpos - kpos
    wm = (dist >= 0) & (dist < WINDOW) & (kpos >= 0)
    sw = jnp.where(wm, _dot(q, kw_t), NEG_INF)
    new = new_ref[0]
    kn = new[:, :_HALF_W]
    vn = new[:, _HALF_W:]
    sn = jnp.sum(q.astype(F32) * kn, axis=-1, keepdims=True)
    mx = jnp.maximum(jnp.max(sw, axis=-1, keepdims=True), sn)
    ew = jnp.where(wm, jnp.exp(sw - mx), 0.0)
    en = jnp.exp(sn - mx)
    den = jnp.sum(ew, axis=-1, keepdims=True) + en
    owin_ref[0] = (_dot_nt(ew.astype(BF16), vw_t) + en * vn) / den


def _nsa_sample_a(qexp, blocks, win_state, new_kvw, qpos):
    bs, n_half, _ = blocks.shape
    w_buf = win_state.shape[-1]
    n_slc = -(-(qpos + 1) // SEL_BLOCK)
    ns_pad = _round_up(n_slc, LANES)
    nh = N_HEADS_NSA
    row3 = lambda bi: (bi, 0, 0)
    body = functools.partial(_nsa_sample_a_body, qpos=qpos, n_half=n_half, n_slc=n_slc,
                             ns_pad=ns_pad, w_buf=w_buf)
    o = jax.ShapeDtypeStruct((bs, nh, LANES), F32)
    oi = jax.ShapeDtypeStruct((bs, nh, LANES), jnp.int32)
    blk = pl.BlockSpec((1, nh, LANES), row3)
    return pl.pallas_call(
        body, grid=(bs,),
        in_specs=[blk, pl.BlockSpec((1, n_half, KV_WIDTH), row3),
                  pl.BlockSpec((1, 2, N_KV_HEADS, HEAD_DIM, w_buf), lambda bi: (bi, 0, 0, 0, 0)),
                  pl.BlockSpec((1, 1, KV_WIDTH), row3)],
        out_specs=(blk, blk, blk, blk), out_shape=(o, o, oi, oi),
        compiler_params=_cparams(("parallel",)),
        name="nsa_sample_cmp_win",
    )(qexp, blocks, win_state, new_kvw)


def _nsa_sample_b_body(idx_ref, ok_ref, pt_ref, q_ref, *refs, n_sel, qpos, n_past_blk, page):
    del pt_ref
    blk_refs = refs[:n_sel]
    new_ref, ocmp_ref, owin_ref, gate_ref, o_ref = refs[n_sel:]
    bi = pl.program_id(0)
    ki = pl.program_id(1)
    per_page = page // SEL_BLOCK
    q = q_ref[0].astype(BF16)
    x = lax.broadcasted_iota(jnp.int32, (1, page), 1)
    scores, vals = [], []
    n_new = jnp.int32(0)
    for jx in range(n_sel):
        bidx = idx_ref[bi, ki, jx]
        good = ok_ref[bi, ki, jx] > 0
        k_t = blk_refs[jx][0, 0, 0].astype(BF16)
        use = good & (bidx < n_past_blk)
        first = jnp.where(use, (bidx % per_page) * SEL_BLOCK, page)
        tok = (bidx // per_page) * page + x
        msk = (x >= first) & (x < first + SEL_BLOCK) & (tok <= qpos)
        scores.append((jnp.where(msk, _dot(q, k_t), NEG_INF), msk))
        vals.append(blk_refs[jx][0, 1, 0].astype(BF16))
        n_new = n_new + jnp.where(good & (bidx == n_past_blk), 1, 0)
    zero = jnp.zeros((1, 1), jnp.int32)
    has_new = (n_new + zero) > 0
    head0 = (ki + zero) == 0
    new = new_ref[0]
    kn = jnp.where(head0, new[:, :HEAD_DIM], new[:, HEAD_DIM:2 * HEAD_DIM])
    vn = jnp.where(head0, new[:, _HALF_W:_HALF_W + HEAD_DIM], new[:, _HALF_W + HEAD_DIM:])
    sn = jnp.where(has_new, jnp.sum(q.astype(F32) * kn, axis=-1, keepdims=True), NEG_INF)
    mx = sn
    for sc, _ in scores:
        mx = jnp.maximum(mx, jnp.max(sc, axis=-1, keepdims=True))
    en = jnp.where(has_new, jnp.exp(sn - mx), 0.0)
    den = en
    acc = en * vn
    for (sc, msk), v_t in zip(scores, vals):
        e = jnp.where(msk, jnp.exp(sc - mx), 0.0)
        den = den + jnp.sum(e, axis=-1, keepdims=True)
        acc = acc + _dot_nt(e.astype(BF16), v_t)
    o_slc = acc * jnp.where(den > 0.0, 1.0 / den, 0.0)
    low = lax.broadcasted_iota(jnp.int32, (N_HEADS_NSA, 1), 0) < GQA_GROUP
    own = lambda y: jnp.where(low, y[:, :HEAD_DIM], y[:, HEAD_DIM:])
    gt = gate_ref[0]
    o = (gt[0, :, :HEAD_DIM] * own(ocmp_ref[0]) + gt[1, :, :HEAD_DIM] * o_slc
         + gt[2, :, :HEAD_DIM] * own(owin_ref[0]))
    o_ref[0, 0] = jnp.where(head0, o[:GQA_GROUP], o[GQA_GROUP:])


def _nsa_sample_b(top_idx, top_ok, page_table, q_rows, cache_t, new_kvs, o_cmp, o_win, gate_exp, qpos):
    bs = q_rows.shape[0]
    page = cache_t.shape[-1]
    per_page = page // SEL_BLOCK
    assert page % SEL_BLOCK == 0
    n_past_blk = page_table.shape[1] * per_page
    n_sel = top_idx.shape[-1]
    nh = N_HEADS_NSA

    def sel_spec(jx):
        def imap(bi, ki, idx, ok, pt):
            blk = jnp.minimum(idx[bi, ki, jx], n_past_blk - 1)
            return (pt[bi, blk // per_page], 0, ki, 0, 0)
        return pl.BlockSpec((1, 2, 1, HEAD_DIM, page), imap)

    row3 = lambda bi, ki, idx, ok, pt: (bi, 0, 0)
    blk = pl.BlockSpec((1, nh, LANES), row3)
    grid_spec = pltpu.PrefetchScalarGridSpec(
        num_scalar_prefetch=3, grid=(bs, N_KV_HEADS),
        in_specs=[pl.BlockSpec((1, nh, HEAD_DIM), row3)] + [sel_spec(jx) for jx in range(n_sel)] + [
            pl.BlockSpec((1, 1, KV_WIDTH), row3), blk, blk,
            pl.BlockSpec((1, N_BRANCH, nh, LANES), lambda bi, ki, idx, ok, pt: (bi, 0, 0, 0))],
        out_specs=pl.BlockSpec((1, 1, GQA_GROUP, HEAD_DIM), lambda bi, ki, idx, ok, pt: (bi, ki, 0, 0)),
    )
    body = functools.partial(_nsa_sample_b_body, n_sel=n_sel, qpos=qpos, n_past_blk=n_past_blk, page=page)
    return pl.pallas_call(
        body, grid_spec=grid_spec,
        out_shape=jax.ShapeDtypeStruct((bs, N_KV_HEADS, GQA_GROUP, HEAD_DIM), F32),
        compiler_params=_cparams(("parallel", "parallel")),
        name="nsa_sample_sel",
    )(top_idx, top_ok, page_table, q_rows, *([cache_t] * n_sel), new_kvs, o_cmp, o_win, gate_exp)


def _hgrn_prompt_body(q_ref, f_ref, v_ref, g_ref, nw_ref, o_ref, s_ref, st_ref, *, tc):
    c = HG_CHUNK
    i = pl.program_id(2)

    @pl.when(i == 0)
    def _():
        st_ref[...] = jnp.zeros(st_ref.shape, F32)

    f = f_ref[...]
    lf = jnp.log(f)
    k = 1.0 - f
    v = v_ref[...]
    r = lax.broadcasted_iota(jnp.int32, (tc, HG_DK), 0) & (c - 1)
    cum = lf
    sh = 1
    while sh < c:
        cum = cum + jnp.where(r >= sh, pltpu.roll(cum, sh, axis=0), 0.0)
        sh *= 2
    q_dec = (q_ref[...] * jnp.exp(cum)).astype(BF16)
    k_inv = (k * jnp.exp(-cum)).astype(BF16)
    causal = (lax.broadcasted_iota(jnp.int32, (c, c), 0) >= lax.broadcasted_iota(jnp.int32, (c, c), 1))
    nw = nw_ref[...]
    st = st_ref[...]
    for ci in range(tc // c):
        sl = slice(ci * c, (ci + 1) * c)
        cum_c = cum[sl]
        last = cum_c[c - 1:c, :]
        vc = v[sl].astype(BF16)
        k_dec = (k[sl] * jnp.exp(last - cum_c)).astype(BF16)
        attn = jnp.where(causal, _dot_nt(q_dec[sl], k_inv[sl]), 0.0)
        o = _dot_nt(q_dec[sl], st.astype(BF16)) + _dot(attn.astype(BF16), vc)
        st = st * jnp.exp(last) + _dot_tn(vc, k_dec)
        o_ref[sl, :] = (_rms(o, nw) * g_ref[sl, :]).astype(o_ref.dtype)
    st_ref[...] = st

    @pl.when(i == pl.num_programs(2) - 1)
    def _():
        s_ref[0, 0] = st.T


def _hgrn_prompt(hq, hf, hv, hg, norm_w, b, t, tc):
    m = b * t
    assert t % tc == 0 and tc % HG_CHUNK == 0
    nt = t // tc
    blk = pl.BlockSpec((tc, HG_DK), lambda bi, hi, i: (bi * nt + i, hi))
    return pl.pallas_call(
        functools.partial(_hgrn_prompt_body, tc=tc),
        grid=(b, HG_HEADS, nt),
        in_specs=[blk, blk, blk, blk, pl.BlockSpec((1, HG_DV), lambda bi, hi, i: (0, hi))],
        out_specs=(blk, pl.BlockSpec((1, 1, HG_DK, HG_DV), lambda bi, hi, i: (bi, hi, 0, 0))),
        out_shape=(jax.ShapeDtypeStruct((m, HG_WIDTH), F32),
                   jax.ShapeDtypeStruct((b, HG_HEADS, HG_DK, HG_DV), F32)),
        scratch_shapes=[pltpu.VMEM((HG_DV, HG_DK), F32)],
        compiler_params=_cparams(("parallel", "parallel", "arbitrary")),
        name="hgrn_prompt",
    )(hq, hf, hv, hg, norm_w.reshape(1, HG_WIDTH))


def _hgrn_sample_body(q_ref, f_ref, v_ref, g_ref, nw_ref, s0_ref, o_ref, s_ref):
    def column(x):
        return jnp.broadcast_to(x, (HG_DV, HG_DK)).T

    for h in range(HG_HEADS):
        sl = slice(h * HG_DK, (h + 1) * HG_DK)
        q = q_ref[0, :, sl]
        f = f_ref[0, :, sl]
        v = v_ref[0, :, sl]
        k = 1.0 - f
        s0 = s0_ref[0, h]
        qk = jnp.sum(q * k, axis=-1, keepdims=True)
        o = jnp.sum(column(q * f) * s0, axis=0, keepdims=True) + qk * v
        s_ref[0, h] = column(f) * s0 + column(k) * v
        o_ref[0, :, sl] = _rms(o, nw_ref[:, sl]) * g_ref[0, :, sl]


def _hgrn_sample(hq, hf, hv, hg, norm_w, s0):
    bs = hq.shape[0]
    r3 = lambda x: x.reshape(bs, 1, HG_WIDTH)
    row = pl.BlockSpec((1, 1, HG_WIDTH), lambda bi: (bi, 0, 0))
    st = pl.BlockSpec((1, HG_HEADS, HG_DK, HG_DV), lambda bi: (bi, 0, 0, 0))
    o, s = pl.pallas_call(
        _hgrn_sample_body, grid=(bs,),
        in_specs=[row, row, row, row, pl.BlockSpec((1, HG_WIDTH), lambda bi: (0, 0)), st],
        out_specs=(row, st),
        out_shape=(jax.ShapeDtypeStruct((bs, 1, HG_WIDTH), F32),
                   jax.ShapeDtypeStruct(s0.shape, F32)),
        compiler_params=_cparams(("parallel",)),
        name="hgrn_sample",
    )(r3(hq), r3(hf), r3(hv), r3(hg), norm_w.reshape(1, HG_WIDTH), s0)
    return o.reshape(bs, HG_WIDTH), s


def _tail_body(h_ref, on_ref, oh_ref, p_ref, gn_ref, wo_ref, gf_ref, wgu_ref, wd_ref, gp_ref,
               wpg_ref, wpp_ref, gl_ref, y_ref, *, d_ff, ff_chunk):
    a = _rms(on_ref[...], gn_ref[...]).astype(BF16)
    b = oh_ref[...].astype(BF16)
    nsa_w = on_ref.shape[-1]
    h = h_ref[...] + _dot(a, wo_ref[:nsa_w, :]) + _dot(b, wo_ref[nsa_w:, :])
    x = _rms(h, gf_ref[...]).astype(BF16)
    for c0 in range(0, d_ff, ff_chunk):
        gate = _dot(x, wgu_ref[:, c0:c0 + ff_chunk])
        up = _dot(x, wgu_ref[:, d_ff + c0:d_ff + c0 + ff_chunk])
        h = h + _dot((_silu(gate) * up).astype(BF16), wd_ref[c0:c0 + ff_chunk, :])
    x = _rms(h, gp_ref[...]).astype(BF16)
    h = h + _sigmoid(_dot(x, wpg_ref[...])) * _dot(p_ref[...].astype(BF16), wpp_ref[...])
    y_ref[...] = _rms(h, gl_ref[...])


def _tail(h, o_nsa, o_hg, p_emb, norm_nsa, w_out, norm_ffn, w_gate_up, w_down, norm_ple, w_ple_gate,
          w_ple_proj, norm_final, tm):
    m, d = h.shape
    assert m % tm == 0
    d_ff = w_down.shape[0]
    ff_chunk = d_ff
    row = lambda i: (i, 0)
    const = lambda i: (0, 0)
    vec = lambda g: g.reshape(1, -1)
    full = lambda w: pl.BlockSpec(w.shape, const)
    blk = lambda x: pl.BlockSpec((tm, x.shape[1]), row)
    args = (h, o_nsa, o_hg, p_emb, vec(norm_nsa), w_out, vec(norm_ffn), w_gate_up, w_down,
            vec(norm_ple), w_ple_gate, w_ple_proj, vec(norm_final))
    in_specs = [blk(h), blk(o_nsa), blk(o_hg), blk(p_emb)] + [full(w) for w in args[4:]]
    return pl.pallas_call(
        functools.partial(_tail_body, d_ff=d_ff, ff_chunk=ff_chunk),
        grid=(m // tm,), in_specs=in_specs, out_specs=pl.BlockSpec((tm, d), row),
        out_shape=jax.ShapeDtypeStruct((m, d), F32),
        compiler_params=_cparams(("parallel",), V7X_VMEM_LIMIT),
        name="tail",
    )(*args)


def _tile(m, pref):
    return pref if m % pref == 0 else m


def kernel(x_prompt, x_sample, cache_kv_cmp, cache_kv_slc, state_kv_win, state_hgrn, page_table,
           p_prompt, p_sample, norm_mix, w_in, cmp_pos, cmp_w1, cmp_w2, hg_lb, norm_nsa_out,
           norm_hg_out, w_out, norm_ffn, w_gate_up, w_down, norm_ple, w_ple_gate, w_ple_proj,
           norm_final):
    depth = w_in.shape[0]
    assert depth == 1
    l = 0
    b, t, d = x_prompt.shape
    bs, ts, _ = x_sample.shape
    assert ts == 1
    n_pool, page = cache_kv_cmp.shape[1:3]
    n_pages = page_table.shape[1]
    past = n_pages * page
    w_buf = state_kv_win.shape[2]
    nkv = N_KV_HEADS

    lb = jnp.cumsum(jax.nn.softmax(hg_lb.astype(F32), axis=0), axis=0)[l]
    w_packed = _pack_w_in(w_in[l])
    wbd, prow, w2bd = _pack_cmp_weights(cmp_pos[l], cmp_w1[l], cmp_w2[l])
    tail_w = (norm_nsa_out[l], w_out[l].astype(BF16), norm_ffn[l], w_gate_up[l].astype(BF16),
              w_down[l].astype(BF16), norm_ple[l], w_ple_gate[l].astype(BF16),
              w_ple_proj[l].astype(BF16), norm_final)

    m = b * t
    xp = x_prompt.reshape(m, d)
    (q_t, kvt, k_rows, v_tiles, gate_t, hq, hf, hv, hg) = _inproj(
        xp, norm_mix[l], w_packed, lb, _tile(m, 256), seq=(b, t))
    kvt = kvt.reshape(N_BRANCH, b, 2, nkv, HEAD_DIM, t)
    pages_p = t // page
    pt_p = jnp.zeros((b, pages_p), jnp.int32)
    blocks_p, blocks_pt = _compress(kvt[0], pt_p, lambda pt, bi, p: (bi, 0, 0, 0, p), page, wbd, prow, w2bd)
    o_nsa_p = _nsa_prompt(q_t, gate_t, blocks_p, blocks_pt, k_rows, v_tiles, b, t, _tile(t, 128))
    o_hg_p, s_p = _hgrn_prompt(hq, hf, hv, hg, norm_hg_out[l], b, t, _tile(t, 256))
    y_p = _tail(xp, o_nsa_p, o_hg_p, p_prompt[l].reshape(m, -1), *tail_w, tm=_tile(m, 256))

    xs = x_sample.reshape(bs, d)
    (q_hs, kv_s, gate_s, hq_s, hf_s, hv_s, hg_s) = _inproj(xs, norm_mix[l], w_packed, lb, bs)
    token_minor = lambda a: a.transpose(0, 2, 3, 4, 1)
    blocks_s, _ = _compress(token_minor(cache_kv_cmp[l]), page_table, lambda pt, bi, p: (pt[bi, p], 0, 0, 0, 0),
                            page, wbd, prow, w2bd)
    q_rows = q_hs.astype(F32).transpose(1, 0, 2)
    lane_half = (jnp.arange(N_HEADS_NSA) // GQA_GROUP)[:, None] == jnp.arange(nkv)[None, :]
    qexp = (q_rows[:, :, None, :] * lane_half[None, :, :, None]).reshape(bs, N_HEADS_NSA, LANES)
    o_cmp_s, o_win_s, idx_s, ok_s = _nsa_sample_a(
        qexp, blocks_s, token_minor(state_kv_win[l]), kv_s[2].reshape(bs, 1, KV_WIDTH), past)
    n_sel = min(SEL_TOPK, -(-(past + 1) // SEL_BLOCK))
    top_idx = idx_s[:, ::GQA_GROUP, :n_sel]
    top_ok = ok_s[:, ::GQA_GROUP, :n_sel]
    gsm = gate_s[:, :, :N_BRANCH * GQA_GROUP].reshape(nkv, bs, N_BRANCH, GQA_GROUP)
    gate_exp = jnp.broadcast_to(gsm.transpose(1, 2, 0, 3).reshape(bs, N_BRANCH, N_HEADS_NSA, 1),
                                (bs, N_BRANCH, N_HEADS_NSA, LANES))
    o_sel = _nsa_sample_b(top_idx, top_ok, page_table, q_rows, token_minor(cache_kv_slc[l]),
                          kv_s[1].reshape(bs, 1, KV_WIDTH), o_cmp_s, o_win_s, gate_exp, past)
    o_nsa_s = o_sel.reshape(bs, NSA_WIDTH)
    o_hg_s, s_s = _hgrn_sample(hq_s, hf_s, hv_s, hg_s, norm_hg_out[l], state_hgrn[l])
    y_s = _tail(xs, o_nsa_s, o_hg_s, p_sample[l].reshape(bs, -1), *tail_w, tm=bs)

    rows_major = lambda a: a.transpose(0, 4, 1, 2, 3)[None]
    kv6 = lambda a: a.reshape(1, bs, 1, 2, nkv, HEAD_DIM)
    win_s = jnp.concatenate([state_kv_win[l], kv_s[2].reshape(bs, 1, 2, nkv, HEAD_DIM)], axis=1)[:, -w_buf:]
    return (y_p.reshape(b, t, d), y_s.reshape(bs, ts, d),
            rows_major(kvt[0]), rows_major(kvt[1]), rows_major(kvt[2][..., t - min(WINDOW, t):]), s_p[None],
            kv6(kv_s[0]), kv6(kv_s[1]), win_s[None], s_s[None])
```
